```python
import math
import jax, jax.numpy as jnp
from jax import lax
import numpy as np


D_MODEL = 1024
BATCH = 16
SEQ = 2048
DEPTH = 1

ATTN_HEADS = 16
ATTN_HEAD_DIM = 64
ATTN_WIDTH = ATTN_HEADS * ATTN_HEAD_DIM
ATTN_SCALE = ATTN_HEAD_DIM ** -0.5
MOBA_BLOCK = 256
MOBA_TOPK = 3
Q_BLOCK = 128
REL_BUCKETS = 32
REL_MAX_EXACT = 16
REL_MAX_DISTANCE = 1024
SSM_EXPAND = 2
SSM_INNER = SSM_EXPAND * D_MODEL
SSM_HEAD_DIM = 64
SSM_HEADS = SSM_INNER // SSM_HEAD_DIM
SSM_GROUPS = 4
SSM_HEADS_PER_GROUP = SSM_HEADS // SSM_GROUPS
SSM_STATE = 128
SSM_CONV = 4
SSM_CHUNK = 128
SSM_CONV_DIM = SSM_INNER + 2 * SSM_GROUPS * SSM_STATE
N_EXPERTS = 32
TOP_K = 4
EXPERT_FF = D_MODEL
SWIGLU_LIMIT = 7.0
SWIGLU_ALPHA = 1.702
MOE_BLOCK = 128
N_BRANCHES = 2
IN_PROJ_DIM = 3 * ATTN_WIDTH + SSM_INNER + SSM_CONV_DIM + SSM_HEADS + N_BRANCHES * D_MODEL
EPS = 1e-6

kernel_name = 'hybrid_moba_mamba2_moe_adaln'


def rms_norm(x, g):
    xf = x.astype(jnp.float32)
    y = xf * lax.rsqrt(jnp.mean(xf * xf, axis=-1, keepdims=True) + EPS)
    return (y * g.astype(jnp.float32)).astype(x.dtype)


def t5_bucket(dist):
    is_small = dist < REL_MAX_EXACT
    d = jnp.maximum(dist, 1).astype(jnp.float32)
    large = REL_MAX_EXACT + (jnp.log(d / REL_MAX_EXACT) / math.log(REL_MAX_DISTANCE / REL_MAX_EXACT)
                             * (REL_BUCKETS - REL_MAX_EXACT)).astype(jnp.int32)
    large = jnp.minimum(large, REL_BUCKETS - 1)
    return jnp.where(is_small, dist, large)


def moba_attention_single(q, k, v, rel_table):
    S = q.shape[0]
    s_pad = -(-S // MOBA_BLOCK) * MOBA_BLOCK
    pad = ((0, s_pad - S), (0, 0), (0, 0))
    q, k, v = (jnp.pad(t, pad).transpose(1, 0, 2) for t in (q, k, v))
    n_blocks = s_pad // MOBA_BLOCK
    n_sel = min(MOBA_TOPK, n_blocks)
    kb = k.reshape(ATTN_HEADS, n_blocks, MOBA_BLOCK, ATTN_HEAD_DIM)
    vb = v.reshape(ATTN_HEADS, n_blocks, MOBA_BLOCK, ATTN_HEAD_DIM)
    k_mean = jnp.mean(kb.astype(jnp.float32), axis=2)
    q_blk = jnp.arange(s_pad) // MOBA_BLOCK
    fully_past = jnp.arange(n_blocks)[None, :] < q_blk[:, None]
    gate = jnp.einsum('hsd,hnd->hsn', q.astype(jnp.float32), k_mean)
    _, sel = lax.top_k(jnp.where(fully_past[None], gate, -jnp.inf), n_sel)
    sel_ok = jnp.arange(n_sel)[None, :] < q_blk[:, None]
    table_t = rel_table.T.astype(jnp.float32)
    h_ix = jnp.arange(ATTN_HEADS)[:, None, None]
    key_off = jnp.arange(MOBA_BLOCK)

    def query_block(args):
        qb, idx, ok, q0 = args
        q_pos = q0 + jnp.arange(Q_BLOCK)
        k_sel = kb[h_ix, idx]
        v_sel = vb[h_ix, idx]
        s_past = jnp.einsum('hqd,hqntd->hqnt', qb, k_sel).astype(jnp.float32) * ATTN_SCALE
        dist_past = q_pos[None, :, None, None] - (idx[..., None] * MOBA_BLOCK + key_off)
        s_past = s_past + table_t[h_ix[..., None], t5_bucket(jnp.maximum(dist_past, 0))]
        own = q0 // MOBA_BLOCK
        k_own = lax.dynamic_index_in_dim(kb, own, axis=1, keepdims=False)
        v_own = lax.dynamic_index_in_dim(vb, own, axis=1, keepdims=False)
        dist_own = q_pos[:, None] - (own * MOBA_BLOCK + key_off)[None, :]
        s_own = (jnp.einsum('hqd,htd->hqt', qb, k_own).astype(jnp.float32) * ATTN_SCALE
                 + table_t[:, t5_bucket(jnp.maximum(dist_own, 0))])
        logits = jnp.concatenate([s_past.reshape(ATTN_HEADS, Q_BLOCK, n_sel * MOBA_BLOCK), s_own], axis=-1)
        mask = jnp.concatenate([
            jnp.broadcast_to(ok[None, :, :, None], s_past.shape).reshape(ATTN_HEADS, Q_BLOCK, n_sel * MOBA_BLOCK),
            jnp.broadcast_to((dist_own >= 0)[None], s_own.shape)], axis=-1)
        p = jax.nn.softmax(jnp.where(mask, logits, -jnp.inf), axis=-1).astype(v_own.dtype)
        p_past = p[..., :n_sel * MOBA_BLOCK].reshape(ATTN_HEADS, Q_BLOCK, n_sel, MOBA_BLOCK)
        p_own = p[..., n_sel * MOBA_BLOCK:]
        return (jnp.einsum('hqnt,hqntd->hqd', p_past, v_sel)
                + jnp.einsum('hqt,htd->hqd', p_own, v_own))

    n_qb = s_pad // Q_BLOCK
    xs = (q.reshape(ATTN_HEADS, n_qb, Q_BLOCK, ATTN_HEAD_DIM).transpose(1, 0, 2, 3),
          sel.reshape(ATTN_HEADS, n_qb, Q_BLOCK, n_sel).transpose(1, 0, 2, 3),
          sel_ok.reshape(n_qb, Q_BLOCK, n_sel),
          jnp.arange(n_qb, dtype=jnp.int32) * Q_BLOCK)
    out = lax.map(query_block, xs)
    return out.transpose(0, 2, 1, 3).reshape(s_pad, ATTN_HEADS, ATTN_HEAD_DIM)[:S]


def causal_depthwise_conv(x, w, b):
    K = w.shape[0]
    S = x.shape[1]
    xp = jnp.pad(x, ((0, 0), (K - 1, 0), (0, 0)))
    y = b
    for tap in range(K):
        y = y + xp[:, tap:tap + S] * w[tap]
    return y


def ssd_chunked(xh, dt, a, b_in, c_in):
    Bsz, S = xh.shape[:2]
    nc = S // SSM_CHUNK
    L, G, J = SSM_CHUNK, SSM_GROUPS, SSM_HEADS_PER_GROUP
    xdt = (xh * dt[..., None]).reshape(Bsz, nc, L, G, J, SSM_HEAD_DIM)
    a_dt = (dt * a).reshape(Bsz, nc, L, G, J).transpose(0, 3, 4, 1, 2)
    a_cs = jnp.cumsum(a_dt, axis=-1)
    bc = b_in.reshape(Bsz, nc, L, G, SSM_STATE)
    cc = c_in.reshape(Bsz, nc, L, G, SSM_STATE)
    pos = jnp.arange(L)
    causal = pos[:, None] >= pos[None, :]
    seg = a_cs[..., :, None] - a_cs[..., None, :]
    decay = jnp.where(causal, jnp.exp(jnp.where(causal, seg, 0.0)), 0.0)
    cb = jnp.einsum('bclgn,bcsgn->bcgls', cc, bc)
    y_diag = jnp.einsum('bcgls,bgjcls,bcsgjp->bclgjp', cb, decay, xdt)
    decay_to_end = jnp.exp(a_cs[..., -1:] - a_cs)
    chunk_states = jnp.einsum('bclgn,bgjcl,bclgjp->bcgjpn', bc, decay_to_end, xdt)
    chunk_decay = jnp.exp(a_cs[..., -1])

    def carry_state(h, inp):
        st, dec = inp
        return h * dec[..., None, None] + st, h

    h0 = jnp.zeros_like(chunk_states[:, 0])
    _, prev = lax.scan(carry_state, h0, (jnp.moveaxis(chunk_states, 1, 0), jnp.moveaxis(chunk_decay, -1, 0)))
    prev = jnp.moveaxis(prev, 0, 1)
    y_off = jnp.einsum('bclgn,bcgjpn,bgjcl->bclgjp', cc, prev, jnp.exp(a_cs))
    return (y_diag + y_off).reshape(Bsz, S, SSM_HEADS, SSM_HEAD_DIM)


def mamba2_branch(z, xbc, dt_raw, conv_w, conv_b, dt_bias, a_log, d_skip, ssm_norm_g):
    Bsz, S, _ = z.shape
    xbc = jax.nn.silu(causal_depthwise_conv(xbc, conv_w, conv_b))
    xs, b_in, c_in = jnp.split(xbc, [SSM_INNER, SSM_INNER + SSM_GROUPS * SSM_STATE], axis=-1)
    xs = xs.reshape(Bsz, S, SSM_HEADS, SSM_HEAD_DIM).astype(jnp.float32)
    dt = jax.nn.softplus((dt_raw + dt_bias).astype(jnp.float32))
    a = -jnp.exp(a_log.astype(jnp.float32))
    y = ssd_chunked(xs, dt, a,
                    b_in.reshape(Bsz, S, SSM_GROUPS, SSM_STATE).astype(jnp.float32),
                    c_in.reshape(Bsz, S, SSM_GROUPS, SSM_STATE).astype(jnp.float32))
    y = y + d_skip.astype(jnp.float32)[:, None] * xs
    y = y.reshape(Bsz, S, SSM_INNER) * jax.nn.silu(z.astype(jnp.float32))
    yg = y.reshape(Bsz, S, SSM_GROUPS, SSM_INNER // SSM_GROUPS)
    yg = yg * lax.rsqrt(jnp.mean(yg * yg, axis=-1, keepdims=True) + EPS)
    return (yg.reshape(Bsz, S, SSM_INNER) * ssm_norm_g.astype(jnp.float32)).astype(z.dtype)


def clamped_swiglu(h):
    glu, lin = jnp.split(h, 2, axis=-1)
    glu = jnp.minimum(glu, SWIGLU_LIMIT)
    lin = jnp.clip(lin, -SWIGLU_LIMIT, SWIGLU_LIMIT)
    return glu * jax.nn.sigmoid(SWIGLU_ALPHA * glu) * (lin + 1.0)


def moe_ffn(u, router_w, router_b, w1, b1, w2, b2):
    T, D = u.shape
    logits = (u @ router_w).astype(jnp.float32) + router_b.astype(jnp.float32)
    top_logits, top_e = lax.top_k(logits, TOP_K)
    top_w = jax.nn.softmax(top_logits, axis=-1).astype(u.dtype)
    n_pairs = T * TOP_K
    e_flat = top_e.reshape(-1)
    tok_flat = jnp.arange(n_pairs, dtype=jnp.int32) // TOP_K
    order = jnp.argsort(e_flat)
    e_sorted = e_flat[order]
    tok_sorted = tok_flat[order]
    w_sorted = top_w.reshape(-1)[order]
    counts = jax.ops.segment_sum(jnp.ones(n_pairs, jnp.int32), e_flat, num_segments=N_EXPERTS)
    padded = (counts + MOE_BLOCK - 1) // MOE_BLOCK * MOE_BLOCK
    start = jnp.cumsum(counts) - counts
    p_end = jnp.cumsum(padded)
    p_start = p_end - padded
    dest = p_start[e_sorted] + jnp.arange(n_pairs, dtype=jnp.int32) - start[e_sorted]
    cap = n_pairs + N_EXPERTS * MOE_BLOCK
    n_blocks = cap // MOE_BLOCK
    buf_tok = jnp.zeros(cap, jnp.int32).at[dest].set(tok_sorted)
    buf_w = jnp.zeros(cap, u.dtype).at[dest].set(w_sorted)
    block_e = jnp.minimum(jnp.searchsorted(p_end, jnp.arange(n_blocks) * MOE_BLOCK, side='right'), N_EXPERTS - 1)

    def expert_block(args):
        tok, wt, e = args
        xb = u[tok]
        h = xb @ w1[e] + b1[e]
        y = clamped_swiglu(h) @ w2[e] + b2[e]
        return y * wt[:, None]

    ys = lax.map(expert_block, (buf_tok.reshape(n_blocks, MOE_BLOCK), buf_w.reshape(n_blocks, MOE_BLOCK), block_e))
    return jnp.zeros_like(u).at[buf_tok].add(ys.reshape(cap, D))


def hybrid_layer(x, c_act, ada_w, ada_b, norm1_g, w_in, q_norm_g, k_norm_g, rel_bias_table,
                 conv_w, conv_b, dt_bias, a_log, d_skip, ssm_norm_g, w_attn_branch, w_ssm_branch,
                 gate_bias, w_out, norm2_g, router_w, router_b, expert_w1, expert_b1, expert_w2, expert_b2):
    Bsz, S, D = x.shape
    mod = c_act @ ada_w + ada_b
    shift1, scale1, gate1, shift2, scale2, gate2 = (m[:, None, :] for m in jnp.split(mod, 6, axis=-1))
    u = rms_norm(x, norm1_g) * (1.0 + scale1) + shift1
    proj = jnp.einsum('bsd,de->bse', u, w_in)
    c0 = 3 * ATTN_WIDTH
    cuts = [ATTN_WIDTH, 2 * ATTN_WIDTH, c0, c0 + SSM_INNER, c0 + SSM_INNER + SSM_CONV_DIM,
            c0 + SSM_INNER + SSM_CONV_DIM + SSM_HEADS]
    q, k, v, z, xbc, dt_raw, gate_logits = jnp.split(proj, cuts, axis=-1)
    heads = (Bsz, S, ATTN_HEADS, ATTN_HEAD_DIM)
    q = rms_norm(q.reshape(heads), q_norm_g)
    k = rms_norm(k.reshape(heads), k_norm_g)
    v = v.reshape(heads)
    attn = lax.map(lambda qkv: moba_attention_single(qkv[0], qkv[1], qkv[2], rel_bias_table), (q, k, v))
    attn = attn.reshape(Bsz, S, ATTN_WIDTH)
    ssm = mamba2_branch(z, xbc, dt_raw, conv_w, conv_b, dt_bias, a_log, d_skip, ssm_norm_g)
    gate_attn, gate_ssm = jnp.split(jax.nn.sigmoid(gate_logits + gate_bias), 2, axis=-1)
    merged = gate_attn * (attn @ w_attn_branch) + gate_ssm * (ssm @ w_ssm_branch)
    x = x + gate1 * (merged @ w_out)
    u2 = rms_norm(x, norm2_g) * (1.0 + scale2) + shift2
    ffn = moe_ffn(u2.reshape(Bsz * S, D), router_w, router_b, expert_w1, expert_b1, expert_w2, expert_b2)
    return x + gate2 * ffn.reshape(Bsz, S, D)


def setup_inputs(seed: int = 0) -> dict:
    key = jax.random.key(seed)
    ks = jax.random.split(key, 32)
    f32 = jnp.float32

    def nrm(k, shape, s):
        return jax.random.normal(k, shape, f32) * s

    L = DEPTH
    dt0 = jnp.exp(jax.random.uniform(ks[11], (L, SSM_HEADS), f32) * (math.log(0.1) - math.log(0.001)) + math.log(0.001))
    return {
        'x': nrm(ks[0], (BATCH, SEQ, D_MODEL), 1.0),
        'c': nrm(ks[1], (BATCH, D_MODEL), 1.0),
        'ada_w': nrm(ks[2], (L, D_MODEL, 6 * D_MODEL), D_MODEL ** -0.5),
        'ada_b': nrm(ks[3], (L, 6 * D_MODEL), 0.02),
        'norm1_g': 1.0 + nrm(ks[4], (L, D_MODEL), 0.02),
        'w_in': nrm(ks[5], (L, D_MODEL, IN_PROJ_DIM), D_MODEL ** -0.5),
        'q_norm_g': 1.0 + nrm(ks[6], (L, ATTN_HEAD_DIM), 0.02),
        'k_norm_g': 1.0 + nrm(ks[7], (L, ATTN_HEAD_DIM), 0.02),
        'rel_bias_table': nrm(ks[8], (REL_BUCKETS, ATTN_HEADS), 0.2),
        'conv_w': nrm(ks[9], (L, SSM_CONV, SSM_CONV_DIM), SSM_CONV ** -0.5),
        'conv_b': nrm(ks[10], (L, SSM_CONV_DIM), 0.02),
        'dt_bias': dt0 + jnp.log(-jnp.expm1(-dt0)),
        'a_log': jnp.log(jax.random.uniform(ks[12], (L, SSM_HEADS), f32, minval=1.0, maxval=16.0)),
        'd_skip': 1.0 + nrm(ks[13], (L, SSM_HEADS), 0.02),
        'ssm_norm_g': 1.0 + nrm(ks[14], (L, SSM_INNER), 0.02),
        'w_attn_branch': nrm(ks[15], (L, ATTN_WIDTH, D_MODEL), ATTN_WIDTH ** -0.5),
        'w_ssm_branch': nrm(ks[16], (L, SSM_INNER, D_MODEL), SSM_INNER ** -0.5),
        'gate_bias': nrm(ks[17], (L, N_BRANCHES * D_MODEL), 0.02),
        'w_out': nrm(ks[18], (L, D_MODEL, D_MODEL), D_MODEL ** -0.5),
        'norm2_g': 1.0 + nrm(ks[19], (L, D_MODEL), 0.02),
        'router_w': nrm(ks[20], (L, D_MODEL, N_EXPERTS), D_MODEL ** -0.5),
        'router_b': nrm(ks[21], (L, N_EXPERTS), 0.01),
        'expert_w1': nrm(ks[22], (L, N_EXPERTS, D_MODEL, 2 * EXPERT_FF), D_MODEL ** -0.5),
        'expert_b1': nrm(ks[23], (L, N_EXPERTS, 2 * EXPERT_FF), 0.01),
        'expert_w2': nrm(ks[24], (L, N_EXPERTS, EXPERT_FF, D_MODEL), EXPERT_FF ** -0.5),
        'expert_b2': nrm(ks[25], (L, N_EXPERTS, D_MODEL), 0.01),
    }


def reference(x, c, ada_w, ada_b, norm1_g, w_in, q_norm_g, k_norm_g, rel_bias_table, conv_w, conv_b,
              dt_bias, a_log, d_skip, ssm_norm_g, w_attn_branch, w_ssm_branch, gate_bias, w_out, norm2_g,
              router_w, router_b, expert_w1, expert_b1, expert_w2, expert_b2):
    c_act = jax.nn.silu(c)
    h = x
    for l in range(DEPTH):
        h = hybrid_layer(h, c_act, ada_w[l], ada_b[l], norm1_g[l], w_in[l], q_norm_g[l], k_norm_g[l],
                         rel_bias_table, conv_w[l], conv_b[l], dt_bias[l], a_log[l], d_skip[l],
                         ssm_norm_g[l], w_attn_branch[l], w_ssm_branch[l], gate_bias[l], w_out[l],
                         norm2_g[l], router_w[l], router_b[l], expert_w1[l], expert_b1[l],
                         expert_w2[l], expert_b2[l])
    return h
```

```python
import functools
import math

import numpy as np
import jax
import jax.numpy as jnp
from jax import lax
from jax.experimental import pallas as pl
from jax.experimental.pallas import tpu as pltpu

F32, BF16, I32 = jnp.float32, jnp.bfloat16, jnp.int32

D = 1024
H = 16
HD = 64
ATTN_W = H * HD
SCALE = HD ** -0.5
BLK = 256
TOPK = 3
NBUCKET = 32
REL_EXACT = 16
REL_MAXD = 1024
SSM_IN = 2048
SSM_HD = 64
SSM_H = SSM_IN // SSM_HD
SSM_G = 4
SSM_J = SSM_H // SSM_G
SSM_N = 128
SSM_K = 4
L = 128
CONV_DIM = SSM_IN + 2 * SSM_G * SSM_N
NE = 32
TOPE = 4
FF = D
LIMIT = 7.0
ALPHA = 1.702
EPS = 1e-6
NEG_INF = float("-inf")

Z0, G0, X0, Q0, K0, V0, BC0 = 0, 2048, 4096, 6144, 7168, 8192, 9216
NPROJ = 10240

MOE_ROWS = 256
GATHER_ROWS = 512
COMBINE_TOK = 256
VMEM_LIMIT = 56 * 1024 * 1024


def _dot(a, b):
    return jnp.dot(a, b, preferred_element_type=F32)


def _dot_nt(a, b):
    return lax.dot_general(a, b, (((1,), (1,)), ((), ())), preferred_element_type=F32)


def _cparams(*sem):
    return pltpu.CompilerParams(dimension_semantics=sem, vmem_limit_bytes=VMEM_LIMIT)


def _sigmoid(x):
    return 1.0 / (1.0 + jnp.exp(-x))


def _softplus(x):
    return jnp.maximum(x, 0.0) + jnp.log1p(jnp.exp(-jnp.abs(x)))


def _ada_body(c_ref, w_ref, b_ref, o_ref):
    c = c_ref[...]
    ca = c * _sigmoid(c)
    o_ref[...] = _dot(ca.astype(BF16), w_ref[...].astype(BF16)) + b_ref[...]


def _ada(c, w, b):
    bsz = c.shape[0]
    n = w.shape[1]
    tn = 1536
    return pl.pallas_call(
        _ada_body,
        grid=(n // tn,),
        in_specs=[pl.BlockSpec((bsz, D), lambda j: (0, 0)),
                  pl.BlockSpec((D, tn), lambda j: (0, j)),
                  pl.BlockSpec((1, tn), lambda j: (0, j))],
        out_specs=pl.BlockSpec((bsz, tn), lambda j: (0, j)),
        out_shape=jax.ShapeDtypeStruct((bsz, n), F32),
        compiler_params=_cparams("arbitrary"),
        name="ada_mod",
    )(c, w, b.reshape(1, n))


def _inproj_body(x_ref, mod_ref, g_ref, w_ref, wdtt_ref, wdt_ref, o_ref, dtt_ref, dt_ref, u_scr):
    @pl.when(pl.program_id(2) == 0)
    def _():
        x = x_ref[0]
        ms = jnp.mean(x * x, axis=-1, keepdims=True)
        xn = x * lax.rsqrt(ms + EPS) * g_ref[...]
        u = xn * (1.0 + mod_ref[0, 1:2, :]) + mod_ref[0, 0:1, :]
        ub = u.astype(BF16)
        u_scr[...] = ub
        dtt_ref[0] = _dot_nt(wdtt_ref[...], ub)
        dt_ref[0] = _dot(ub, wdt_ref[...])

    o_ref[0] = _dot_nt(w_ref[...], u_scr[...]).astype(BF16)


def _inproj(x, mod, g1, wt, wdtt, wdt):
    bsz, s, _ = x.shape
    tm = min(1024, s)
    tn = 512
    return pl.pallas_call(
        _inproj_body,
        grid=(bsz, s // tm, NPROJ // tn),
        in_specs=[pl.BlockSpec((1, tm, D), lambda b, m, n: (b, m, 0)),
                  pl.BlockSpec((1, 6, D), lambda b, m, n: (b, 0, 0)),
                  pl.BlockSpec((1, D), lambda b, m, n: (0, 0)),
                  pl.BlockSpec((tn, D), lambda b, m, n: (n, 0)),
                  pl.BlockSpec((SSM_H, D), lambda b, m, n: (0, 0)),
                  pl.BlockSpec((D, SSM_H), lambda b, m, n: (0, 0))],
        out_specs=[pl.BlockSpec((1, tn, tm), lambda b, m, n: (b, n, m)),
                   pl.BlockSpec((1, SSM_H, tm), lambda b, m, n: (b, 0, m)),
                   pl.BlockSpec((1, tm, SSM_H), lambda b, m, n: (b, m, 0))],
        out_shape=[jax.ShapeDtypeStruct((bsz, NPROJ, s), BF16),
                   jax.ShapeDtypeStruct((bsz, SSM_H, s), F32),
                   jax.ShapeDtypeStruct((bsz, s, SSM_H), F32)],
        scratch_shapes=[pltpu.VMEM((tm, D), BF16)],
        compiler_params=_cparams("arbitrary", "arbitrary", "arbitrary"),
        name="in_proj",
    )(x, mod, g1, wt, wdtt, wdt)


def _t5_thresholds(max_dist):
    d = np.arange(max_dist, dtype=np.int64)
    df = np.maximum(d, 1).astype(np.float32)
    large = REL_EXACT + (np.log(df / np.float32(REL_EXACT)) / np.float32(math.log(REL_MAXD / REL_EXACT))
                         * np.float32(NBUCKET - REL_EXACT)).astype(np.int32)
    bucket = np.where(d < REL_EXACT, d, np.minimum(large, NBUCKET - 1))
    return [int(np.argmax(bucket >= b)) if np.any(bucket >= b) else int(max_dist) for b in range(NBUCKET)]


def _bias_body(tab_ref, o_ref, *, thr):
    h = pl.program_id(0)
    delta = pl.program_id(1)
    ki = lax.broadcasted_iota(I32, (BLK, BLK), 0)
    qi = lax.broadcasted_iota(I32, (BLK, BLK), 1)
    d = jnp.maximum(delta * BLK + qi - ki, 0)
    val = jnp.full((BLK, BLK), tab_ref[h, 0], F32)
    for b in range(1, NBUCKET):
        val = jnp.where(d >= thr[b], tab_ref[h, b], val)
    o_ref[0, 0] = val


def _bias_tiles(table_t, nb):
    thr = _t5_thresholds(nb * BLK + BLK)
    return pl.pallas_call(
        functools.partial(_bias_body, thr=thr),
        grid=(H, nb),
        in_specs=[pl.BlockSpec(memory_space=pltpu.SMEM)],
        out_specs=pl.BlockSpec((1, 1, BLK, BLK), lambda h, d: (h, d, 0, 0)),
        out_shape=jax.ShapeDtypeStruct((H, nb, BLK, BLK), F32),
        compiler_params=_cparams("arbitrary", "arbitrary"),
        name="t5_bias_tiles",
    )(table_t)


def _attn_body(q_ref, k_ref, v_ref, bias_ref, qg_ref, kg_ref, o_ref, k_scr, v_scr, selb_scr, *, s):
    nb = s // BLK

    def normed(ref, g_ref, hh):
        t = ref[0, hh * HD:(hh + 1) * HD, :].astype(F32)
        ms = jnp.mean(t * t, axis=0, keepdims=True)
        return t * lax.rsqrt(ms + EPS) * g_ref[...]

    qn = [normed(q_ref, qg_ref, hh) for hh in range(2)]
    kn = [normed(k_ref, kg_ref, hh) for hh in range(2)]
    kpair = jnp.concatenate(kn, axis=0)
    for j in range(nb):
        k_scr[j] = kpair[:, j * BLK:(j + 1) * BLK].T.astype(BF16)
        v_scr[j] = v_ref[0, :, j * BLK:(j + 1) * BLK]

    ki = lax.broadcasted_iota(I32, (BLK, BLK), 0)
    qi = lax.broadcasted_iota(I32, (BLK, BLK), 1)
    causal_neg = jnp.where(qi >= ki, 0.0, NEG_INF)
    zeros_q = jnp.zeros((HD, BLK), F32)

    for hh in range(2):
        gate = []
        for j in range(nb):
            kmean = jnp.mean(kn[hh][:, j * BLK:(j + 1) * BLK], axis=1, keepdims=True)
            gate.append(jnp.sum(qn[hh] * kmean, axis=0, keepdims=True))
        for i in range(1, nb):
            rows = [g[:, i * BLK:(i + 1) * BLK] for g in gate[:i]]
            for j in range(i):
                if i <= TOPK:
                    sel = jnp.zeros((1, BLK), F32)
                else:
                    rank = jnp.zeros((1, BLK), F32)
                    for jj in range(i):
                        if jj == j:
                            continue
                        beats = (rows[jj] >= rows[j]) if jj < j else (rows[jj] > rows[j])
                        rank = rank + jnp.where(beats, 1.0, 0.0)
                    sel = jnp.where(rank < TOPK, 0.0, NEG_INF)
                selb_scr[hh, i, j:j + 1, :] = sel

        bias_own = bias_ref[hh, 0] + causal_neg
        qs = qn[hh] * SCALE
        for i in range(nb):
            qblk = qs[:, i * BLK:(i + 1) * BLK]
            qpad = jnp.concatenate([qblk, zeros_q] if hh == 0 else [zeros_q, qblk], axis=0).astype(BF16)
            sc = _dot(k_scr[i], qpad) + bias_own
            m = jnp.max(sc, axis=0, keepdims=True)
            p = jnp.exp(sc - m)
            l = jnp.sum(p, axis=0, keepdims=True)
            acc = _dot(v_scr[i, hh * HD:(hh + 1) * HD, :], p.astype(BF16))

            def past(j, carry, i=i, hh=hh, qpad=qpad):
                m, l, acc = carry
                sc = _dot(k_scr[j], qpad) + bias_ref[hh, i - j] + selb_scr[hh, i, pl.ds(j, 1), :]
                m_new = jnp.maximum(m, jnp.max(sc, axis=0, keepdims=True))
                alpha = jnp.exp(m - m_new)
                p = jnp.exp(sc - m_new)
                l = alpha * l + jnp.sum(p, axis=0, keepdims=True)
                acc = alpha * acc + _dot(v_scr[j, hh * HD:(hh + 1) * HD, :], p.astype(BF16))
                return m_new, l, acc

            if i > 0:
                m, l, acc = lax.fori_loop(0, i, past, (m, l, acc))
            o_ref[0, hh * HD:(hh + 1) * HD, i * BLK:(i + 1) * BLK] = (acc / l).astype(BF16)


def _attention(projt, bias, qg, kg):
    bsz, _, s = projt.shape
    nb = s // BLK
    rows = 2 * HD
    qb, kb, vb = Q0 // rows, K0 // rows, V0 // rows
    return pl.pallas_call(
        functools.partial(_attn_body, s=s),
        grid=(H // 2, bsz),
        in_specs=[pl.BlockSpec((1, rows, s), lambda p, b: (b, qb + p, 0)),
                  pl.BlockSpec((1, rows, s), lambda p, b: (b, kb + p, 0)),
                  pl.BlockSpec((1, rows, s), lambda p, b: (b, vb + p, 0)),
                  pl.BlockSpec((2, nb, BLK, BLK), lambda p, b: (p, 0, 0, 0)),
                  pl.BlockSpec((HD, 1), lambda p, b: (0, 0)),
                  pl.BlockSpec((HD, 1), lambda p, b: (0, 0))],
        out_specs=pl.BlockSpec((1, rows, s), lambda p, b: (b, p, 0)),
        out_shape=jax.ShapeDtypeStruct((bsz, ATTN_W, s), BF16),
        scratch_shapes=[pltpu.VMEM((nb, BLK, rows), BF16),
                        pltpu.VMEM((nb, rows, BLK), BF16),
                        pltpu.VMEM((2, nb, nb, BLK), F32)],
        compiler_params=_cparams("arbitrary", "arbitrary"),
        name="moba_attention",
    )(projt, projt, projt, bias, qg, kg)


def _split3(x):
    hi = x.astype(BF16)
    r1 = x - hi.astype(F32)
    mid = r1.astype(BF16)
    lo = (r1 - mid.astype(F32)).astype(BF16)
    return hi, mid, lo


def _ssd_body(z_ref, xs_ref, bc_ref, dtt_ref, dt_ref, cw_ref, cb_ref, dtbc_ref, dtbr_ref, alc_ref, alr_ref,
              dskip_ref, ng_ref, o_ref, prevx_scr, state_scr):
    @pl.when(pl.program_id(1) == 0)
    def _():
        prevx_scr[...] = jnp.zeros_like(prevx_scr)
        state_scr[...] = jnp.zeros_like(state_scr)

    cur = jnp.concatenate([xs_ref[0], bc_ref[0]], axis=0).astype(F32)
    prev = prevx_scr[...]
    lane = lax.broadcasted_iota(I32, (1, L), 1)
    acc = cb_ref[...] + cw_ref[SSM_K - 1] * cur
    for sft in range(1, SSM_K):
        shifted = jnp.where(lane >= sft, pltpu.roll(cur, sft, 1), pltpu.roll(prev, sft, 1))
        acc = acc + cw_ref[SSM_K - 1 - sft] * shifted
    prevx_scr[...] = cur
    xbc = acc * _sigmoid(acc)

    dtt = _softplus(dtt_ref[0] + dtbc_ref[...])
    dts = _softplus(dt_ref[0] + dtbr_ref[...])
    adtt = dtt * (-jnp.exp(alc_ref[...]))
    adts = dts * (-jnp.exp(alr_ref[...]))
    r_i = lax.broadcasted_iota(I32, (L, L), 0)
    c_i = lax.broadcasted_iota(I32, (L, L), 1)
    upper = jnp.where(r_i <= c_i, 1.0, 0.0).astype(BF16)
    lower = jnp.where(r_i >= c_i, 1.0, 0.0).astype(BF16)
    acst = sum(_dot(part, upper) for part in _split3(adtt))
    acs = sum(_dot(lower, part) for part in _split3(adts))
    causal = c_i >= r_i
    last = acst[:, L - 1:L]
    dte = jnp.exp(last - acst)
    cdec = jnp.broadcast_to(jnp.exp(last), (SSM_H, L))
    ein = jnp.exp(acst)

    for g in range(SSM_G):
        bt = xbc[SSM_IN + g * SSM_N:SSM_IN + (g + 1) * SSM_N]
        ct = xbc[SSM_IN + SSM_G * SSM_N + g * SSM_N:SSM_IN + SSM_G * SSM_N + (g + 1) * SSM_N]
        ctb = ct.astype(BF16)
        bmb = bt.T.astype(BF16)
        cbt = _dot(bmb, ctb)
        rows = slice(g * SSM_J * SSM_HD, (g + 1) * SSM_J * SSM_HD)
        xg = xbc[rows]
        state = state_scr[rows]
        yoff = _dot(state.astype(BF16), ctb)
        xdt, sin, cd, ys = [], [], [], []
        for j in range(SSM_J):
            h = g * SSM_J + j
            xj = xg[j * SSM_HD:(j + 1) * SSM_HD]
            xdj = xj * dtt[h:h + 1]
            xdt.append(xdj)
            sin.append(xdj * dte[h:h + 1])
            cd.append(jnp.broadcast_to(cdec[h:h + 1], (SSM_HD, SSM_N)))
        new_state = _dot(jnp.concatenate(sin, axis=0).astype(BF16), bmb)
        state_scr[rows] = state * jnp.concatenate(cd, axis=0) + new_state
        for j in range(SSM_J):
            h = g * SSM_J + j
            seg = acst[h:h + 1] - acs[:, h:h + 1]
            mt = cbt * jnp.exp(jnp.where(causal, seg, NEG_INF))
            ydiag = _dot(xdt[j].astype(BF16), mt.astype(BF16))
            ys.append(ydiag + yoff[j * SSM_HD:(j + 1) * SSM_HD] * ein[h:h + 1])
        y = jnp.concatenate(ys, axis=0) + dskip_ref[rows] * xg
        zg = z_ref[0, rows].astype(F32)
        y = y * (zg * _sigmoid(zg))
        ms = jnp.mean(y * y, axis=0, keepdims=True)
        o_ref[0, rows] = (y * lax.rsqrt(ms + EPS) * ng_ref[rows]).astype(BF16)


def _ssd(projt, dtt, dts, cw, cb, dtb, alog, dskip, ng):
    bsz, _, s = projt.shape
    nc = s // L
    full = lambda shape: pl.BlockSpec(shape, lambda b, c: (0,) * len(shape))
    return pl.pallas_call(
        _ssd_body,
        grid=(bsz, nc),
        in_specs=[pl.BlockSpec((1, SSM_IN, L), lambda b, c: (b, Z0 // SSM_IN, c)),
                  pl.BlockSpec((1, SSM_IN, L), lambda b, c: (b, X0 // SSM_IN, c)),
                  pl.BlockSpec((1, 2 * SSM_G * SSM_N, L), lambda b, c: (b, BC0 // (2 * SSM_G * SSM_N), c)),
                  pl.BlockSpec((1, SSM_H, L), lambda b, c: (b, 0, c)),
                  pl.BlockSpec((1, L, SSM_H), lambda b, c: (b, c, 0)),
                  full((SSM_K, CONV_DIM, L)), full((CONV_DIM, L)),
                  full((SSM_H, 1)), full((1, SSM_H)), full((SSM_H, 1)), full((1, SSM_H)),
                  full((SSM_IN, L)), full((SSM_IN, L))],
        out_specs=pl.BlockSpec((1, SSM_IN, L), lambda b, c: (b, 0, c)),
        out_shape=jax.ShapeDtypeStruct((bsz, SSM_IN, s), BF16),
        scratch_shapes=[pltpu.VMEM((CONV_DIM, L), F32), pltpu.VMEM((SSM_IN, SSM_N), F32)],
        compiler_params=_cparams("arbitrary", "arbitrary"),
        name="ssd_branch",
    )(projt, projt, projt, dtt, dts, cw, cb, dtb.reshape(SSM_H, 1), dtb.reshape(1, SSM_H),
      alog.reshape(SSM_H, 1), alog.reshape(1, SSM_H), dskip, ng)


def _merge_body(attn_ref, ssm_ref, gate_ref, x_ref, mod_ref, gb_ref, wat_ref, wst_ref, wo_ref, g2_ref,
                rwt_ref, rb_ref, x1_ref, u2_ref, lg_ref):
    a = _dot(wat_ref[...], attn_ref[0])
    sm = _dot(wst_ref[...], ssm_ref[0])
    gate = _sigmoid(gate_ref[0].astype(F32) + gb_ref[...])
    merged_t = gate[:D] * a + gate[D:] * sm
    merged = merged_t.T.astype(BF16)
    y = _dot(merged, wo_ref[...])
    x1 = x_ref[0] + mod_ref[0, 2:3, :] * y
    x1_ref[0] = x1
    ms = jnp.mean(x1 * x1, axis=-1, keepdims=True)
    u2 = (x1 * lax.rsqrt(ms + EPS) * g2_ref[...]) * (1.0 + mod_ref[0, 4:5, :]) + mod_ref[0, 3:4, :]
    u2_ref[0] = u2
    lg_ref[...] = _dot_nt(rwt_ref[...], u2.astype(BF16)) + rb_ref[...]


def _merge(attnt, ssmt, projt, x, mod, gb, wat, wst, wo, g2, rwt, rb):
    bsz, s, _ = x.shape
    tm = 512
    nm = s // tm
    full = lambda shape: pl.BlockSpec(shape, lambda b, m: (0,) * len(shape))
    return pl.pallas_call(
        _merge_body,
        grid=(bsz, nm),
        in_specs=[pl.BlockSpec((1, ATTN_W, tm), lambda b, m: (b, 0, m)),
                  pl.BlockSpec((1, SSM_IN, tm), lambda b, m: (b, 0, m)),
                  pl.BlockSpec((1, 2 * D, tm), lambda b, m: (b, G0 // (2 * D), m)),
                  pl.BlockSpec((1, tm, D), lambda b, m: (b, m, 0)),
                  pl.BlockSpec((1, 6, D), lambda b, m: (b, 0, 0)),
                  full((2 * D, 1)), full((D, ATTN_W)), full((D, SSM_IN)), full((D, D)), full((1, D)),
                  full((NE, D)), full((NE, 1))],
        out_specs=[pl.BlockSpec((1, tm, D), lambda b, m: (b, m, 0)),
                   pl.BlockSpec((1, tm, D), lambda b, m: (b, m, 0)),
                   pl.BlockSpec((NE, tm), lambda b, m: (0, b * nm + m))],
        out_shape=[jax.ShapeDtypeStruct((bsz, s, D), F32),
                   jax.ShapeDtypeStruct((bsz, s, D), F32),
                   jax.ShapeDtypeStruct((NE, bsz * s), F32)],
        compiler_params=_cparams("arbitrary", "arbitrary"),
        name="merge_outproj",
    )(attnt, ssmt, projt, x, mod, gb, wat, wst, wo, g2, rwt, rb)


def _route_body(lg_ref, e_ref, w_ref, r_ref, cnt_ref, carry_scr, *, tm):
    @pl.when(pl.program_id(0) == 0)
    def _():
        carry_scr[...] = jnp.zeros_like(carry_scr)

    work = lg_ref[...]
    eidx = lax.broadcasted_iota(I32, (NE, tm), 0).astype(F32)
    vals, idxs, hots = [], [], []
    for _ in range(TOPE):
        m = jnp.max(work, axis=0, keepdims=True)
        idx = jnp.min(jnp.where(work == m, eidx, float(NE)), axis=0, keepdims=True)
        hot = eidx == idx
        work = jnp.where(hot, NEG_INF, work)
        vals.append(m)
        idxs.append(idx)
        hots.append(hot)
    ex = [jnp.exp(v - vals[0]) for v in vals]
    den = ex[0] + ex[1] + ex[2] + ex[3]
    multi = jnp.zeros((NE, tm), F32)
    for hot in hots:
        multi = multi + jnp.where(hot, 1.0, 0.0)
    r_i = lax.broadcasted_iota(I32, (tm, tm), 0)
    c_i = lax.broadcasted_iota(I32, (tm, tm), 1)
    strict = jnp.where(r_i < c_i, 1.0, 0.0).astype(BF16)
    before = _dot(multi.astype(BF16), strict) + carry_scr[:, 0:1]
    for k in range(TOPE):
        e_ref[k:k + 1, :] = idxs[k].astype(I32)
        w_ref[k:k + 1, :] = ex[k] / den
        r_ref[k:k + 1, :] = jnp.sum(jnp.where(hots[k], before, 0.0), axis=0, keepdims=True).astype(I32)
    total = carry_scr[...] + jnp.sum(multi, axis=1, keepdims=True)
    carry_scr[...] = total
    cnt_ref[...] = total


def _route(logits_t):
    t = logits_t.shape[1]
    tm = 512
    return pl.pallas_call(
        functools.partial(_route_body, tm=tm),
        grid=(t // tm,),
        in_specs=[pl.BlockSpec((NE, tm), lambda i: (0, i))],
        out_specs=[pl.BlockSpec((TOPE, tm), lambda i: (0, i)),
                   pl.BlockSpec((TOPE, tm), lambda i: (0, i)),
                   pl.BlockSpec((TOPE, tm), lambda i: (0, i)),
                   pl.BlockSpec((NE, 128), lambda i: (0, 0))],
        out_shape=[jax.ShapeDtypeStruct((TOPE, t), I32),
                   jax.ShapeDtypeStruct((TOPE, t), F32),
                   jax.ShapeDtypeStruct((TOPE, t), I32),
                   jax.ShapeDtypeStruct((NE, 128), F32)],
        scratch_shapes=[pltpu.VMEM((NE, 128), F32)],
        compiler_params=_cparams("arbitrary"),
        name="route_top4",
    )(logits_t)


def _row_copy(src_hbm, src_row, dst, dst_row, sem):
    return pltpu.make_async_copy(src_hbm.at[pl.ds(src_row, 1)], dst.at[pl.ds(dst_row, 1)], sem)


def _gather_body(tok_ref, u_hbm, o_hbm, sem):
    base = pl.program_id(0) * GATHER_ROWS

    def start(r, c):
        _row_copy(u_hbm, tok_ref[0, 0, r], o_hbm, base + r, sem).start()
        return c

    def wait(r, c):
        _row_copy(u_hbm, 0, o_hbm, base, sem).wait()
        return c

    lax.fori_loop(0, GATHER_ROWS, start, 0)
    lax.fori_loop(0, GATHER_ROWS, wait, 0)


def _gather_rows(u2, src_tok):
    cap = src_tok.shape[0]
    steps = cap // GATHER_ROWS
    return pl.pallas_call(
        _gather_body,
        grid=(steps,),
        in_specs=[pl.BlockSpec((1, 1, GATHER_ROWS), lambda i: (i, 0, 0), memory_space=pltpu.SMEM),
                  pl.BlockSpec(memory_space=pl.ANY)],
        out_specs=pl.BlockSpec(memory_space=pl.ANY),
        out_shape=jax.ShapeDtypeStruct((cap, D), F32),
        scratch_shapes=[pltpu.SemaphoreType.DMA(())],
        compiler_params=_cparams("arbitrary"),
        name="moe_gather",
    )(src_tok.reshape(steps, 1, GATHER_ROWS), u2)


def _expert_body(be_ref, nu_ref, x_ref, w1_ref, b1_ref, w2_ref, b2_ref, o_ref, w1_scr, w2_scr):
    i = pl.program_id(0)
    changed = jnp.logical_or(i == 0, be_ref[i] != be_ref[jnp.maximum(i - 1, 0)])

    @pl.when(changed)
    def _():
        w1_scr[...] = w1_ref[0].astype(BF16)
        w2_scr[...] = w2_ref[0].astype(BF16)

    @pl.when(i < nu_ref[0])
    def _():
        h = _dot(x_ref[...].astype(BF16), w1_scr[...]) + b1_ref[0]
        glu = jnp.minimum(h[:, :FF], LIMIT)
        lin = jnp.clip(h[:, FF:], -LIMIT, LIMIT)
        act = glu * _sigmoid(ALPHA * glu) * (lin + 1.0)
        o_ref[...] = _dot(act.astype(BF16), w2_scr[...]) + b2_ref[0]

    @pl.when(i >= nu_ref[0])
    def _():
        o_ref[...] = jnp.zeros_like(o_ref)


def _experts(xs, block_e, n_used, w1, b1, w2, b2):
    cap = xs.shape[0]
    nblk = cap // MOE_ROWS
    grid_spec = pltpu.PrefetchScalarGridSpec(
        num_scalar_prefetch=2,
        grid=(nblk,),
        in_specs=[pl.BlockSpec((MOE_ROWS, D), lambda i, be, nu: (i, 0)),
                  pl.BlockSpec((1, D, 2 * FF), lambda i, be, nu: (be[i], 0, 0)),
                  pl.BlockSpec((1, 1, 2 * FF), lambda i, be, nu: (be[i], 0, 0)),
                  pl.BlockSpec((1, FF, D), lambda i, be, nu: (be[i], 0, 0)),
                  pl.BlockSpec((1, 1, D), lambda i, be, nu: (be[i], 0, 0))],
        out_specs=pl.BlockSpec((MOE_ROWS, D), lambda i, be, nu: (i, 0)),
        scratch_shapes=[pltpu.VMEM((D, 2 * FF), BF16), pltpu.VMEM((FF, D), BF16)],
    )
    return pl.pallas_call(
        _expert_body,
        grid_spec=grid_spec,
        out_shape=jax.ShapeDtypeStruct((cap, D), F32),
        compiler_params=_cparams("arbitrary"),
        name="moe_experts",
    )(block_e, n_used, xs, w1, b1.reshape(NE, 1, 2 * FF), w2, b2.reshape(NE, 1, D))


def _combine_body(dest_ref, ys_hbm, w_ref, x1_ref, g2_ref, o_ref, buf, sem):
    def start(r, c):
        for k in range(TOPE):
            _row_copy(ys_hbm, dest_ref[0, k, r], buf.at[k], r, sem).start()
        return c

    def wait(r, c):
        for k in range(TOPE):
            _row_copy(ys_hbm, 0, buf.at[k], 0, sem).wait()
        return c

    lax.fori_loop(0, COMBINE_TOK, start, 0)
    lax.fori_loop(0, COMBINE_TOK, wait, 0)
    w = w_ref[...]
    ffn = buf[0] * w[:, 0:1]
    for k in range(1, TOPE):
        ffn = ffn + buf[k] * w[:, k:k + 1]
    o_ref[0] = x1_ref[0] + g2_ref[0] * ffn


def _combine(ys, dest, top_w, x1, gate2):
    bsz, s, _ = x1.shape
    tm = COMBINE_TOK
    nm = s // tm
    steps = bsz * nm
    dest_blocks = dest.reshape(TOPE, steps, tm).transpose(1, 0, 2)
    return pl.pallas_call(
        _combine_body,
        grid=(bsz, nm),
        in_specs=[pl.BlockSpec((1, TOPE, tm), lambda b, m: (b * nm + m, 0, 0), memory_space=pltpu.SMEM),
                  pl.BlockSpec(memory_space=pl.ANY),
                  pl.BlockSpec((tm, TOPE), lambda b, m: (b * nm + m, 0)),
                  pl.BlockSpec((1, tm, D), lambda b, m: (b, m, 0)),
                  pl.BlockSpec((1, 1, D), lambda b, m: (b, 0, 0))],
        out_specs=pl.BlockSpec((1, tm, D), lambda b, m: (b, m, 0)),
        out_shape=jax.ShapeDtypeStruct((bsz, s, D), F32),
        scratch_shapes=[pltpu.VMEM((TOPE, tm, D), F32), pltpu.SemaphoreType.DMA(())],
        compiler_params=_cparams("arbitrary", "arbitrary"),
        name="moe_combine",
    )(dest_blocks, ys, top_w.T, x1, gate2)


def _layer(x, mod, norm1_g, w_in, q_norm_g, k_norm_g, rel_bias_table, conv_w, conv_b, dt_bias, a_log, d_skip,
           ssm_norm_g, w_attn_branch, w_ssm_branch, gate_bias, w_out, norm2_g, router_w, router_b,
           expert_w1, expert_b1, expert_w2, expert_b2):
    bsz, s, _ = x.shape
    t = bsz * s
    assert s % BLK == 0 and s % 512 == 0 and t % GATHER_ROWS == 0
    mod3 = mod.reshape(bsz, 6, D)

    c0 = 3 * ATTN_W
    c_x, c_b, c_dt, c_g = c0 + SSM_IN, c0 + SSM_IN + SSM_IN, c0 + SSM_IN + CONV_DIM, c0 + SSM_IN + CONV_DIM + SSM_H
    w_rows = jnp.concatenate([w_in[:, c0:c_x], w_in[:, c_g:], w_in[:, c_x:c_b], w_in[:, :c0], w_in[:, c_b:c_dt]],
                             axis=1)
    wt = w_rows.T.astype(BF16)
    wdt = w_in[:, c_dt:c_g].astype(BF16)
    projt, dtt, dts = _inproj(x, mod3, norm1_g.reshape(1, D), wt, wdt.T, wdt)

    bias = _bias_tiles(rel_bias_table.T.astype(F32), s // BLK)
    attnt = _attention(projt, bias, q_norm_g.reshape(HD, 1), k_norm_g.reshape(HD, 1))

    lanes = lambda v: jnp.broadcast_to(v[..., None], v.shape + (L,))
    ssmt = _ssd(projt, dtt, dts, lanes(conv_w), lanes(conv_b), dt_bias, a_log,
                lanes(jnp.repeat(d_skip, SSM_HD)), lanes(ssm_norm_g))

    x1, u2, logits_t = _merge(attnt, ssmt, projt, x, mod3, gate_bias.reshape(2 * D, 1),
                              w_attn_branch.T.astype(BF16), w_ssm_branch.T.astype(BF16), w_out.astype(BF16),
                              norm2_g.reshape(1, D), router_w.T.astype(BF16), router_b.reshape(NE, 1))

    top_e, top_w, rank, cnt = _route(logits_t)
    counts = cnt[:, 0].astype(I32)
    padded = (counts + MOE_ROWS - 1) // MOE_ROWS * MOE_ROWS
    p_end = jnp.cumsum(padded)
    p_start = p_end - padded
    dest = p_start[top_e] + rank
    cap = TOPE * t + NE * MOE_ROWS
    tok = jnp.broadcast_to(jnp.arange(t, dtype=I32)[None], (TOPE, t))
    src_tok = jnp.zeros((cap,), I32).at[dest.reshape(-1)].set(tok.reshape(-1))
    nblk = cap // MOE_ROWS
    block_e = jnp.minimum(jnp.searchsorted(p_end, jnp.arange(nblk, dtype=I32) * MOE_ROWS, side="right"),
                          NE - 1).astype(I32)
    n_used = (p_end[-1] // MOE_ROWS).astype(I32).reshape(1)

    xs = _gather_rows(u2.reshape(t, D), src_tok)
    ys = _experts(xs, block_e, n_used, expert_w1, expert_b1, expert_w2, expert_b2)
    return _combine(ys, dest, top_w, x1, mod3[:, 5:6, :])


def kernel(x, c, ada_w, ada_b, norm1_g, w_in, q_norm_g, k_norm_g, rel_bias_table, conv_w, conv_b, dt_bias, a_log,
           d_skip, ssm_norm_g, w_attn_branch, w_ssm_branch, gate_bias, w_out, norm2_g, router_w, router_b,
           expert_w1, expert_b1, expert_w2, expert_b2):
    h = x
    for l in range(ada_w.shape[0]):
        mod = _ada(c, ada_w[l], ada_b[l])
        h = _layer(h, mod, norm1_g[l], w_in[l], q_norm_g[l], k_norm_g[l], rel_bias_table, conv_w[l], conv_b[l],
                   dt_bias[l], a_log[l], d_skip[l], ssm_norm_g[l], w_attn_branch[l], w_ssm_branch[l],
                   gate_bias[l], w_out[l], norm2_g[l], router_w[l], router_b[l], expert_w1[l], expert_b1[l],
                   expert_w2[l], expert_b2[l])
    return h
```

```python
import functools
import math

import numpy as np
import jax
import jax.numpy as jnp
from jax import lax
from jax.experimental import pallas as pl
from jax.experimental.pallas import tpu as pltpu

F32, BF16, I32 = jnp.float32, jnp.bfloat16, jnp.int32

D = 1024
H = 16
HD = 64
ATTN_W = H * HD
SCALE = HD ** -0.5
BLK = 256
TOPK = 3
NBUCKET = 32
REL_EXACT = 16
REL_MAXD = 1024
SSM_IN = 2048
SSM_HD = 64
SSM_H = SSM_IN // SSM_HD
SSM_G = 4
SSM_J = SSM_H // SSM_G
SSM_N = 128
SSM_K = 4
L = 128
CONV_DIM = SSM_IN + 2 * SSM_G * SSM_N
NE = 32
TOPE = 4
FF = D
LIMIT = 7.0
ALPHA = 1.702
EPS = 1e-6
NEG_INF = float("-inf")

Z0, G0, X0, Q0, K0, V0, BC0 = 0, 2048, 4096, 6144, 7168, 8192, 9216
NPROJ = 10240

MOE_ROWS = 256
COMBINE_TOK = 256
VMEM_LIMIT = 56 * 1024 * 1024


def _dot(a, b):
    return jnp.dot(a, b, preferred_element_type=F32)


def _dot_nt(a, b):
    return lax.dot_general(a, b, (((1,), (1,)), ((), ())), preferred_element_type=F32)


def _cparams(*sem):
    return pltpu.CompilerParams(dimension_semantics=sem, vmem_limit_bytes=VMEM_LIMIT)


def _sigmoid(x):
    return 1.0 / (1.0 + jnp.exp(-x))


def _softplus(x):
    return jnp.maximum(x, 0.0) + jnp.log1p(jnp.exp(-jnp.abs(x)))


def _ada_body(c_ref, w_ref, b_ref, o_ref):
    c = c_ref[...]
    ca = c * _sigmoid(c)
    o_ref[...] = _dot(ca.astype(BF16), w_ref[...].astype(BF16)) + b_ref[...]


def _ada(c, w, b):
    bsz = c.shape[0]
    n = w.shape[1]
    tn = 1536
    return pl.pallas_call(
        _ada_body,
        grid=(n // tn,),
        in_specs=[pl.BlockSpec((bsz, D), lambda j: (0, 0)),
                  pl.BlockSpec((D, tn), lambda j: (0, j)),
                  pl.BlockSpec((1, tn), lambda j: (0, j))],
        out_specs=pl.BlockSpec((bsz, tn), lambda j: (0, j)),
        out_shape=jax.ShapeDtypeStruct((bsz, n), F32),
        compiler_params=_cparams("arbitrary"),
        name="ada_mod",
    )(c, w, b.reshape(1, n))


def _inproj_body(x_ref, mod_ref, g_ref, w_ref, wdtt_ref, wdt_ref, o_ref, dtt_ref, dt_ref, u_scr):
    @pl.when(pl.program_id(2) == 0)
    def _():
        x = x_ref[0]
        ms = jnp.mean(x * x, axis=-1, keepdims=True)
        xn = x * lax.rsqrt(ms + EPS) * g_ref[...]
        u = xn * (1.0 + mod_ref[0, 1:2, :]) + mod_ref[0, 0:1, :]
        ub = u.astype(BF16)
        u_scr[...] = ub
        dtt_ref[0] = _dot_nt(wdtt_ref[...], ub)
        dt_ref[0] = _dot(ub, wdt_ref[...])

    o_ref[0] = _dot_nt(w_ref[...], u_scr[...]).astype(BF16)


def _inproj(x, mod, g1, wt, wdtt, wdt):
    bsz, s, _ = x.shape
    tm = min(1024, s)
    tn = 512
    return pl.pallas_call(
        _inproj_body,
        grid=(bsz, s // tm, NPROJ // tn),
        in_specs=[pl.BlockSpec((1, tm, D), lambda b, m, n: (b, m, 0)),
                  pl.BlockSpec((1, 6, D), lambda b, m, n: (b, 0, 0)),
                  pl.BlockSpec((1, D), lambda b, m, n: (0, 0)),
                  pl.BlockSpec((tn, D), lambda b, m, n: (n, 0)),
                  pl.BlockSpec((SSM_H, D), lambda b, m, n: (0, 0)),
                  pl.BlockSpec((D, SSM_H), lambda b, m, n: (0, 0))],
        out_specs=[pl.BlockSpec((1, tn, tm), lambda b, m, n: (b, n, m)),
                   pl.BlockSpec((1, SSM_H, tm), lambda b, m, n: (b, 0, m)),
                   pl.BlockSpec((1, tm, SSM_H), lambda b, m, n: (b, m, 0))],
        out_shape=[jax.ShapeDtypeStruct((bsz, NPROJ, s), BF16),
                   jax.ShapeDtypeStruct((bsz, SSM_H, s), F32),
                   jax.ShapeDtypeStruct((bsz, s, SSM_H), F32)],
        scratch_shapes=[pltpu.VMEM((tm, D), BF16)],
        compiler_params=_cparams("arbitrary", "arbitrary", "arbitrary"),
        name="in_proj",
    )(x, mod, g1, wt, wdtt, wdt)


def _t5_thresholds(max_dist):
    d = np.arange(max_dist, dtype=np.int64)
    df = np.maximum(d, 1).astype(np.float32)
    large = REL_EXACT + (np.log(df / np.float32(REL_EXACT)) / np.float32(math.log(REL_MAXD / REL_EXACT))
                         * np.float32(NBUCKET - REL_EXACT)).astype(np.int32)
    bucket = np.where(d < REL_EXACT, d, np.minimum(large, NBUCKET - 1))
    return [int(np.argmax(bucket >= b)) if np.any(bucket >= b) else int(max_dist) for b in range(NBUCKET)]


def _bias_body(tab_ref, o_ref, *, thr):
    h = pl.program_id(0)
    delta = pl.program_id(1)
    ki = lax.broadcasted_iota(I32, (BLK, BLK), 0)
    qi = lax.broadcasted_iota(I32, (BLK, BLK), 1)
    d = jnp.maximum(delta * BLK + qi - ki, 0)
    val = jnp.full((BLK, BLK), tab_ref[h, 0], F32)
    for b in range(1, NBUCKET):
        val = jnp.where(d >= thr[b], tab_ref[h, b], val)
    o_ref[0, 0] = jnp.where(jnp.logical_or(delta > 0, qi >= ki), val, NEG_INF)


def _bias_tiles(table_t, nb):
    thr = _t5_thresholds(nb * BLK + BLK)
    return pl.pallas_call(
        functools.partial(_bias_body, thr=thr),
        grid=(H, nb),
        in_specs=[pl.BlockSpec(memory_space=pltpu.SMEM)],
        out_specs=pl.BlockSpec((1, 1, BLK, BLK), lambda h, d: (h, d, 0, 0)),
        out_shape=jax.ShapeDtypeStruct((H, nb, BLK, BLK), F32),
        compiler_params=_cparams("arbitrary", "arbitrary"),
        name="t5_bias_tiles",
    )(table_t)


def _attn_body(q_ref, k_ref, v_ref, bias_ref, qg_ref, kg_ref, o_ref, k_scr, sc_scr, p_scr, *, s):
    nb = s // BLK

    def normed(ref, g_ref, hh):
        t = ref[0, hh * HD:(hh + 1) * HD, :].astype(F32)
        ms = jnp.mean(t * t, axis=0, keepdims=True)
        return t * lax.rsqrt(ms + EPS) * g_ref[...]

    qn = [normed(q_ref, qg_ref, hh) for hh in range(2)]
    kn = [normed(k_ref, kg_ref, hh) for hh in range(2)]
    kpair = jnp.concatenate(kn, axis=0)
    for j in range(nb):
        k_scr[j] = kpair[:, j * BLK:(j + 1) * BLK].T.astype(BF16)
    zeros_q = jnp.zeros((HD, BLK), F32)

    for hh in range(2):
        gate = []
        for j in range(nb):
            kmean = jnp.mean(kn[hh][:, j * BLK:(j + 1) * BLK], axis=1, keepdims=True)
            gate.append(jnp.sum(qn[hh] * kmean, axis=0, keepdims=True))
        qs = qn[hh] * SCALE
        for i in range(nb):
            sel = [None] * i
            if i > TOPK:
                rows = [g[:, i * BLK:(i + 1) * BLK] for g in gate[:i]]
                for j in range(i):
                    rank = jnp.zeros((1, BLK), F32)
                    for jj in range(i):
                        if jj != j:
                            beats = (rows[jj] >= rows[j]) if jj < j else (rows[jj] > rows[j])
                            rank = rank + jnp.where(beats, 1.0, 0.0)
                    sel[j] = jnp.where(rank < TOPK, 0.0, NEG_INF)
            qblk = qs[:, i * BLK:(i + 1) * BLK]
            qpad = jnp.concatenate([qblk, zeros_q] if hh == 0 else [zeros_q, qblk], axis=0).astype(BF16)
            m = None
            for j in range(i + 1):
                sc = _dot(k_scr[j], qpad) + bias_ref[hh, i - j]
                if j < i and sel[j] is not None:
                    sc = sc + sel[j]
                sc_scr[j] = sc
                mj = jnp.max(sc, axis=0, keepdims=True)
                m = mj if m is None else jnp.maximum(m, mj)
            l = jnp.zeros((1, BLK), F32)
            for j in range(i + 1):
                p = jnp.exp(sc_scr[j] - m)
                l = l + jnp.sum(p, axis=0, keepdims=True)
                p_scr[j * BLK:(j + 1) * BLK, :] = p.astype(BF16)
            acc = _dot(v_ref[0, hh * HD:(hh + 1) * HD, 0:(i + 1) * BLK], p_scr[0:(i + 1) * BLK, :])
            o_ref[0, hh * HD:(hh + 1) * HD, i * BLK:(i + 1) * BLK] = (acc / l).astype(BF16)


def _attention(projt, bias, qg, kg):
    bsz, _, s = projt.shape
    nb = s // BLK
    rows = 2 * HD
    qb, kb, vb = Q0 // rows, K0 // rows, V0 // rows
    return pl.pallas_call(
        functools.partial(_attn_body, s=s),
        grid=(H // 2, bsz),
        in_specs=[pl.BlockSpec((1, rows, s), lambda p, b: (b, qb + p, 0)),
                  pl.BlockSpec((1, rows, s), lambda p, b: (b, kb + p, 0)),
                  pl.BlockSpec((1, rows, s), lambda p, b: (b, vb + p, 0)),
                  pl.BlockSpec((2, nb, BLK, BLK), lambda p, b: (p, 0, 0, 0)),
                  pl.BlockSpec((HD, 1), lambda p, b: (0, 0)),
                  pl.BlockSpec((HD, 1), lambda p, b: (0, 0))],
        out_specs=pl.BlockSpec((1, rows, s), lambda p, b: (b, p, 0)),
        out_shape=jax.ShapeDtypeStruct((bsz, ATTN_W, s), BF16),
        scratch_shapes=[pltpu.VMEM((nb, BLK, rows), BF16),
                        pltpu.VMEM((nb, BLK, BLK), F32),
                        pltpu.VMEM((s, BLK), BF16)],
        compiler_params=_cparams("arbitrary", "arbitrary"),
        name="moba_attention",
    )(projt, projt, projt, bias, qg, kg)


def _split3(x):
    hi = x.astype(BF16)
    r1 = x - hi.astype(F32)
    mid = r1.astype(BF16)
    lo = (r1 - mid.astype(F32)).astype(BF16)
    return hi, mid, lo


def _ssd_body(z_ref, xs_ref, bc_ref, dtt_ref, dt_ref, cw_ref, cb_ref, dtbc_ref, dtbr_ref, alc_ref, alr_ref,
              dskip_ref, ng_ref, o_ref, prevx_scr, state_scr):
    @pl.when(pl.program_id(1) == 0)
    def _():
        prevx_scr[...] = jnp.zeros_like(prevx_scr)
        state_scr[...] = jnp.zeros_like(state_scr)

    cur = jnp.concatenate([xs_ref[0], bc_ref[0]], axis=0).astype(F32)
    prev = prevx_scr[...]
    lane = lax.broadcasted_iota(I32, (1, L), 1)
    acc = cb_ref[...] + cw_ref[SSM_K - 1] * cur
    for sft in range(1, SSM_K):
        shifted = jnp.where(lane >= sft, pltpu.roll(cur, sft, 1), pltpu.roll(prev, sft, 1))
        acc = acc + cw_ref[SSM_K - 1 - sft] * shifted
    prevx_scr[...] = cur
    xbc = acc * _sigmoid(acc)

    dtt = _softplus(dtt_ref[0] + dtbc_ref[...])
    dts = _softplus(dt_ref[0] + dtbr_ref[...])
    adtt = dtt * (-jnp.exp(alc_ref[...]))
    adts = dts * (-jnp.exp(alr_ref[...]))
    r_i = lax.broadcasted_iota(I32, (L, L), 0)
    c_i = lax.broadcasted_iota(I32, (L, L), 1)
    upper = jnp.where(r_i <= c_i, 1.0, 0.0).astype(BF16)
    lower = jnp.where(r_i >= c_i, 1.0, 0.0).astype(BF16)
    acst = sum(_dot(part, upper) for part in _split3(adtt))
    acs = sum(_dot(lower, part) for part in _split3(adts))
    causal = c_i >= r_i
    last = acst[:, L - 1:L]
    dte = jnp.exp(last - acst)
    cdec = jnp.broadcast_to(jnp.exp(last), (SSM_H, L))
    ein = jnp.exp(acst)

    for g in range(SSM_G):
        bt = xbc[SSM_IN + g * SSM_N:SSM_IN + (g + 1) * SSM_N]
        ct = xbc[SSM_IN + SSM_G * SSM_N + g * SSM_N:SSM_IN + SSM_G * SSM_N + (g + 1) * SSM_N]
        ctb = ct.astype(BF16)
        bmb = bt.T.astype(BF16)
        cbt = _dot(bmb, ctb)
        rows = slice(g * SSM_J * SSM_HD, (g + 1) * SSM_J * SSM_HD)
        xg = xbc[rows]
        state = state_scr[rows]
        yoff = _dot(state.astype(BF16), ctb)
        xdt, sin, cd, ys = [], [], [], []
        for j in range(SSM_J):
            h = g * SSM_J + j
            xj = xg[j * SSM_HD:(j + 1) * SSM_HD]
            xdj = xj * dtt[h:h + 1]
            xdt.append(xdj)
            sin.append(xdj * dte[h:h + 1])
            cd.append(jnp.broadcast_to(cdec[h:h + 1], (SSM_HD, SSM_N)))
        new_state = _dot(jnp.concatenate(sin, axis=0).astype(BF16), bmb)
        state_scr[rows] = state * jnp.concatenate(cd, axis=0) + new_state
        for j in range(SSM_J):
            h = g * SSM_J + j
            seg = acst[h:h + 1] - acs[:, h:h + 1]
            mt = cbt * jnp.exp(jnp.where(causal, seg, NEG_INF))
            ydiag = _dot(xdt[j].astype(BF16), mt.astype(BF16))
            ys.append(ydiag + yoff[j * SSM_HD:(j + 1) * SSM_HD] * ein[h:h + 1])
        y = jnp.concatenate(ys, axis=0) + dskip_ref[rows] * xg
        zg = z_ref[0, rows].astype(F32)
        y = y * (zg * _sigmoid(zg))
        ms = jnp.mean(y * y, axis=0, keepdims=True)
        o_ref[0, rows] = (y * lax.rsqrt(ms + EPS) * ng_ref[rows]).astype(BF16)


def _ssd(projt, dtt, dts, cw, cb, dtb, alog, dskip, ng):
    bsz, _, s = projt.shape
    nc = s // L
    full = lambda shape: pl.BlockSpec(shape, lambda b, c: (0,) * len(shape))
    return pl.pallas_call(
        _ssd_body,
        grid=(bsz, nc),
        in_specs=[pl.BlockSpec((1, SSM_IN, L), lambda b, c: (b, Z0 // SSM_IN, c)),
                  pl.BlockSpec((1, SSM_IN, L), lambda b, c: (b, X0 // SSM_IN, c)),
                  pl.BlockSpec((1, 2 * SSM_G * SSM_N, L), lambda b, c: (b, BC0 // (2 * SSM_G * SSM_N), c)),
                  pl.BlockSpec((1, SSM_H, L), lambda b, c: (b, 0, c)),
                  pl.BlockSpec((1, L, SSM_H), lambda b, c: (b, c, 0)),
                  full((SSM_K, CONV_DIM, L)), full((CONV_DIM, L)),
                  full((SSM_H, 1)), full((1, SSM_H)), full((SSM_H, 1)), full((1, SSM_H)),
                  full((SSM_IN, L)), full((SSM_IN, L))],
        out_specs=pl.BlockSpec((1, SSM_IN, L), lambda b, c: (b, 0, c)),
        out_shape=jax.ShapeDtypeStruct((bsz, SSM_IN, s), BF16),
        scratch_shapes=[pltpu.VMEM((CONV_DIM, L), F32), pltpu.VMEM((SSM_IN, SSM_N), F32)],
        compiler_params=_cparams("arbitrary", "arbitrary"),
        name="ssd_branch",
    )(projt, projt, projt, dtt, dts, cw, cb, dtb.reshape(SSM_H, 1), dtb.reshape(1, SSM_H),
      alog.reshape(SSM_H, 1), alog.reshape(1, SSM_H), dskip, ng)


def _merge_body(attn_ref, ssm_ref, gate_ref, x_ref, mod_ref, gb_ref, wat_ref, wst_ref, wo_ref, g2_ref,
                rwt_ref, rb_ref, x1_ref, u2_ref, lg_ref):
    a = _dot(wat_ref[...], attn_ref[0])
    sm = _dot(wst_ref[...], ssm_ref[0])
    gate = _sigmoid(gate_ref[0].astype(F32) + gb_ref[...])
    merged_t = gate[:D] * a + gate[D:] * sm
    merged = merged_t.T.astype(BF16)
    y = _dot(merged, wo_ref[...])
    x1 = x_ref[0] + mod_ref[0, 2:3, :] * y
    x1_ref[0] = x1
    ms = jnp.mean(x1 * x1, axis=-1, keepdims=True)
    u2 = (x1 * lax.rsqrt(ms + EPS) * g2_ref[...]) * (1.0 + mod_ref[0, 4:5, :]) + mod_ref[0, 3:4, :]
    u2_ref[0] = u2
    lg_ref[...] = _dot_nt(rwt_ref[...], u2.astype(BF16)) + rb_ref[...]


def _merge(attnt, ssmt, projt, x, mod, gb, wat, wst, wo, g2, rwt, rb):
    bsz, s, _ = x.shape
    tm = 512
    nm = s // tm
    full = lambda shape: pl.BlockSpec(shape, lambda b, m: (0,) * len(shape))
    return pl.pallas_call(
        _merge_body,
        grid=(bsz, nm),
        in_specs=[pl.BlockSpec((1, ATTN_W, tm), lambda b, m: (b, 0, m)),
                  pl.BlockSpec((1, SSM_IN, tm), lambda b, m: (b, 0, m)),
                  pl.BlockSpec((1, 2 * D, tm), lambda b, m: (b, G0 // (2 * D), m)),
                  pl.BlockSpec((1, tm, D), lambda b, m: (b, m, 0)),
                  pl.BlockSpec((1, 6, D), lambda b, m: (b, 0, 0)),
                  full((2 * D, 1)), full((D, ATTN_W)), full((D, SSM_IN)), full((D, D)), full((1, D)),
                  full((NE, D)), full((NE, 1))],
        out_specs=[pl.BlockSpec((1, tm, D), lambda b, m: (b, m, 0)),
                   pl.BlockSpec((1, tm, D), lambda b, m: (b, m, 0)),
                   pl.BlockSpec((NE, tm), lambda b, m: (0, b * nm + m))],
        out_shape=[jax.ShapeDtypeStruct((bsz, s, D), F32),
                   jax.ShapeDtypeStruct((bsz, s, D), F32),
                   jax.ShapeDtypeStruct((NE, bsz * s), F32)],
        compiler_params=_cparams("arbitrary", "arbitrary"),
        name="merge_outproj",
    )(attnt, ssmt, projt, x, mod, gb, wat, wst, wo, g2, rwt, rb)


def _route_body(lg_ref, e_ref, w_ref, r_ref, cnt_ref, carry_scr, *, tm):
    @pl.when(pl.program_id(0) == 0)
    def _():
        carry_scr[...] = jnp.zeros_like(carry_scr)

    work = lg_ref[...]
    eidx = lax.broadcasted_iota(I32, (NE, tm), 0).astype(F32)
    vals, idxs, hots = [], [], []
    for _ in range(TOPE):
        m = jnp.max(work, axis=0, keepdims=True)
        idx = jnp.min(jnp.where(work == m, eidx, float(NE)), axis=0, keepdims=True)
        hot = eidx == idx
        work = jnp.where(hot, NEG_INF, work)
        vals.append(m)
        idxs.append(idx)
        hots.append(hot)
    ex = [jnp.exp(v - vals[0]) for v in vals]
    den = ex[0] + ex[1] + ex[2] + ex[3]
    multi = jnp.zeros((NE, tm), F32)
    for hot in hots:
        multi = multi + jnp.where(hot, 1.0, 0.0)
    r_i = lax.broadcasted_iota(I32, (tm, tm), 0)
    c_i = lax.broadcasted_iota(I32, (tm, tm), 1)
    strict = jnp.where(r_i < c_i, 1.0, 0.0).astype(BF16)
    before = _dot(multi.astype(BF16), strict) + carry_scr[:, 0:1]
    for k in range(TOPE):
        e_ref[k:k + 1, :] = idxs[k].astype(I32)
        w_ref[k:k + 1, :] = ex[k] / den
        r_ref[k:k + 1, :] = jnp.sum(jnp.where(hots[k], before, 0.0), axis=0, keepdims=True).astype(I32)
    total = carry_scr[...] + jnp.sum(multi, axis=1, keepdims=True)
    carry_scr[...] = total
    cnt_ref[...] = total


def _route(logits_t):
    t = logits_t.shape[1]
    tm = 512
    return pl.pallas_call(
        functools.partial(_route_body, tm=tm),
        grid=(t // tm,),
        in_specs=[pl.BlockSpec((NE, tm), lambda i: (0, i))],
        out_specs=[pl.BlockSpec((TOPE, tm), lambda i: (0, i)),
                   pl.BlockSpec((TOPE, tm), lambda i: (0, i)),
                   pl.BlockSpec((TOPE, tm), lambda i: (0, i)),
                   pl.BlockSpec((NE, 128), lambda i: (0, 0))],
        out_shape=[jax.ShapeDtypeStruct((TOPE, t), I32),
                   jax.ShapeDtypeStruct((TOPE, t), F32),
                   jax.ShapeDtypeStruct((TOPE, t), I32),
                   jax.ShapeDtypeStruct((NE, 128), F32)],
        scratch_shapes=[pltpu.VMEM((NE, 128), F32)],
        compiler_params=_cparams("arbitrary"),
        name="route_top4",
    )(logits_t)


def _row_copy(src_hbm, src_row, dst, dst_row, sem):
    return pltpu.make_async_copy(src_hbm.at[pl.ds(src_row, 1)], dst.at[pl.ds(dst_row, 1)], sem)


def _dispatch_body(pe_ref, nu_ref, dest_ref, u_ref, o_hbm, zero_scr, sem, *, nblk):
    def fill(blk):
        return pltpu.make_async_copy(zero_scr, o_hbm.at[pl.ds(blk * MOE_ROWS, MOE_ROWS)], sem)

    @pl.when(pl.program_id(0) == 0)
    def _():
        zero_scr[...] = jnp.zeros_like(zero_scr)
        pad_blocks = []
        for e in range(NE):
            prev_end = pe_ref[e - 1] if e > 0 else 0
            pad_blocks.append((pe_ref[e] > prev_end, pe_ref[e] // MOE_ROWS - 1))
        for b in range(NE):
            pad_blocks.append((nblk - 1 - b >= nu_ref[0], nblk - 1 - b))
        for cond, blk in pad_blocks:
            @pl.when(cond)
            def _(blk=blk):
                fill(blk).start()
        for cond, blk in pad_blocks:
            @pl.when(cond)
            def _(blk=blk):
                fill(blk).wait()

    def start(r, c):
        for k in range(TOPE):
            _row_copy(u_ref, r, o_hbm, dest_ref[0, k, r], sem).start()
        return c

    def wait(r, c):
        for k in range(TOPE):
            _row_copy(u_ref, 0, o_hbm, 0, sem).wait()
        return c

    lax.fori_loop(0, COMBINE_TOK, start, 0)
    lax.fori_loop(0, COMBINE_TOK, wait, 0)


def _dispatch(u2, dest_blocks, p_end, n_used, cap):
    t = u2.shape[0]
    tm = COMBINE_TOK
    nblk = cap // MOE_ROWS
    grid_spec = pltpu.PrefetchScalarGridSpec(
        num_scalar_prefetch=2,
        grid=(t // tm,),
        in_specs=[pl.BlockSpec((1, TOPE, tm), lambda i, pe, nu: (i, 0, 0), memory_space=pltpu.SMEM),
                  pl.BlockSpec((tm, D), lambda i, pe, nu: (i, 0))],
        out_specs=pl.BlockSpec(memory_space=pl.ANY),
        scratch_shapes=[pltpu.VMEM((MOE_ROWS, D), F32), pltpu.SemaphoreType.DMA(())],
    )
    return pl.pallas_call(
        functools.partial(_dispatch_body, nblk=nblk),
        grid_spec=grid_spec,
        out_shape=jax.ShapeDtypeStruct((cap, D), F32),
        compiler_params=_cparams("arbitrary"),
        name="moe_dispatch",
    )(p_end, n_used, dest_blocks, u2)


def _expert_body(be_ref, nu_ref, x_ref, w1_ref, b1_ref, w2_ref, b2_ref, o_ref, w1_scr, w2_scr):
    i = pl.program_id(0)
    changed = jnp.logical_or(i == 0, be_ref[i] != be_ref[jnp.maximum(i - 1, 0)])

    @pl.when(changed)
    def _():
        w1_scr[...] = w1_ref[0].astype(BF16)
        w2_scr[...] = w2_ref[0].astype(BF16)

    @pl.when(i < nu_ref[0])
    def _():
        h = _dot(x_ref[...].astype(BF16), w1_scr[...]) + b1_ref[0]
        glu = jnp.minimum(h[:, :FF], LIMIT)
        lin = jnp.clip(h[:, FF:], -LIMIT, LIMIT)
        act = glu * _sigmoid(ALPHA * glu) * (lin + 1.0)
        o_ref[...] = _dot(act.astype(BF16), w2_scr[...]) + b2_ref[0]

    @pl.when(i >= nu_ref[0])
    def _():
        o_ref[...] = jnp.zeros_like(o_ref)


def _experts(xs, block_e, n_used, w1, b1, w2, b2):
    cap = xs.shape[0]
    nblk = cap // MOE_ROWS
    grid_spec = pltpu.PrefetchScalarGridSpec(
        num_scalar_prefetch=2,
        grid=(nblk,),
        in_specs=[pl.BlockSpec((MOE_ROWS, D), lambda i, be, nu: (i, 0)),
                  pl.BlockSpec((1, D, 2 * FF), lambda i, be, nu: (be[i], 0, 0)),
                  pl.BlockSpec((1, 1, 2 * FF), lambda i, be, nu: (be[i], 0, 0)),
                  pl.BlockSpec((1, FF, D), lambda i, be, nu: (be[i], 0, 0)),
                  pl.BlockSpec((1, 1, D), lambda i, be, nu: (be[i], 0, 0))],
        out_specs=pl.BlockSpec((MOE_ROWS, D), lambda i, be, nu: (i, 0)),
        scratch_shapes=[pltpu.VMEM((D, 2 * FF), BF16), pltpu.VMEM((FF, D), BF16)],
    )
    return pl.pallas_call(
        _expert_body,
        grid_spec=grid_spec,
        out_shape=jax.ShapeDtypeStruct((cap, D), F32),
        compiler_params=_cparams("arbitrary"),
        name="moe_experts",
    )(block_e, n_used, xs, w1, b1.reshape(NE, 1, 2 * FF), w2, b2.reshape(NE, 1, D))


def _combine_body(dest_ref, ys_hbm, w_ref, x1_ref, g2_ref, o_ref, buf, sem):
    def start(r, c):
        for k in range(TOPE):
            _row_copy(ys_hbm, dest_ref[0, k, r], buf.at[k], r, sem).start()
        return c

    def wait(r, c):
        for k in range(TOPE):
            _row_copy(ys_hbm, 0, buf.at[k], 0, sem).wait()
        return c

    lax.fori_loop(0, COMBINE_TOK, start, 0)
    lax.fori_loop(0, COMBINE_TOK, wait, 0)
    w = w_ref[...]
    ffn = buf[0] * w[:, 0:1]
    for k in range(1, TOPE):
        ffn = ffn + buf[k] * w[:, k:k + 1]
    o_ref[0] = x1_ref[0] + g2_ref[0] * ffn


def _combine(ys, dest_blocks, top_w, x1, gate2):
    bsz, s, _ = x1.shape
    tm = COMBINE_TOK
    nm = s // tm
    return pl.pallas_call(
        _combine_body,
        grid=(bsz, nm),
        in_specs=[pl.BlockSpec((1, TOPE, tm), lambda b, m: (b * nm + m, 0, 0), memory_space=pltpu.SMEM),
                  pl.BlockSpec(memory_space=pl.ANY),
                  pl.BlockSpec((tm, TOPE), lambda b, m: (b * nm + m, 0)),
                  pl.BlockSpec((1, tm, D), lambda b, m: (b, m, 0)),
                  pl.BlockSpec((1, 1, D), lambda b, m: (b, 0, 0))],
        out_specs=pl.BlockSpec((1, tm, D), lambda b, m: (b, m, 0)),
        out_shape=jax.ShapeDtypeStruct((bsz, s, D), F32),
        scratch_shapes=[pltpu.VMEM((TOPE, tm, D), F32), pltpu.SemaphoreType.DMA(())],
        compiler_params=_cparams("arbitrary", "arbitrary"),
        name="moe_combine",
    )(dest_blocks, ys, top_w.T, x1, gate2)


def _layer(x, mod, norm1_g, w_in, q_norm_g, k_norm_g, rel_bias_table, conv_w, conv_b, dt_bias, a_log, d_skip,
           ssm_norm_g, w_attn_branch, w_ssm_branch, gate_bias, w_out, norm2_g, router_w, router_b,
           expert_w1, expert_b1, expert_w2, expert_b2):
    bsz, s, _ = x.shape
    t = bsz * s
    assert s % BLK == 0 and s % 512 == 0
    mod3 = mod.reshape(bsz, 6, D)

    c0 = 3 * ATTN_W
    c_x, c_b, c_dt, c_g = c0 + SSM_IN, c0 + SSM_IN + SSM_IN, c0 + SSM_IN + CONV_DIM, c0 + SSM_IN + CONV_DIM + SSM_H
    w_rows = jnp.concatenate([w_in[:, c0:c_x], w_in[:, c_g:], w_in[:, c_x:c_b], w_in[:, :c0], w_in[:, c_b:c_dt]],
                             axis=1)
    wt = w_rows.T.astype(BF16)
    wdt = w_in[:, c_dt:c_g].astype(BF16)
    projt, dtt, dts = _inproj(x, mod3, norm1_g.reshape(1, D), wt, wdt.T, wdt)

    bias = _bias_tiles(rel_bias_table.T.astype(F32), s // BLK)
    attnt = _attention(projt, bias, q_norm_g.reshape(HD, 1), k_norm_g.reshape(HD, 1))

    lanes = lambda v: jnp.broadcast_to(v[..., None], v.shape + (L,))
    ssmt = _ssd(projt, dtt, dts, lanes(conv_w), lanes(conv_b), dt_bias, a_log,
                lanes(jnp.repeat(d_skip, SSM_HD)), lanes(ssm_norm_g))

    x1, u2, logits_t = _merge(attnt, ssmt, projt, x, mod3, gate_bias.reshape(2 * D, 1),
                              w_attn_branch.T.astype(BF16), w_ssm_branch.T.astype(BF16), w_out.astype(BF16),
                              norm2_g.reshape(1, D), router_w.T.astype(BF16), router_b.reshape(NE, 1))

    top_e, top_w, rank, cnt = _route(logits_t)
    counts = cnt[:, 0].astype(I32)
    padded = (counts + MOE_ROWS - 1) // MOE_ROWS * MOE_ROWS
    p_end = jnp.cumsum(padded)
    p_start = p_end - padded
    experts = jnp.arange(NE, dtype=I32)
    dest = rank + jnp.sum(jnp.where(top_e[..., None] == experts, p_start, 0), axis=-1)
    dest_blocks = dest.reshape(TOPE, t // COMBINE_TOK, COMBINE_TOK).transpose(1, 0, 2)
    cap = TOPE * t + NE * MOE_ROWS
    nblk = cap // MOE_ROWS
    blk_start = jnp.arange(nblk, dtype=I32) * MOE_ROWS
    block_e = jnp.minimum(jnp.sum((p_end[None, :] <= blk_start[:, None]).astype(I32), axis=1), NE - 1)
    n_used = (p_end[-1] // MOE_ROWS).astype(I32).reshape(1)

    xs = _dispatch(u2.reshape(t, D), dest_blocks, p_end.astype(I32), n_used, cap)
    ys = _experts(xs, block_e, n_used, expert_w1, expert_b1, expert_w2, expert_b2)
    return _combine(ys, dest_blocks, top_w, x1, mod3[:, 5:6, :])


def kernel(x, c, ada_w, ada_b, norm1_g, w_in, q_norm_g, k_norm_g, rel_bias_table, conv_w, conv_b, dt_bias, a_log,
           d_skip, ssm_norm_g, w_attn_branch, w_ssm_branch, gate_bias, w_out, norm2_g, router_w, router_b,
           expert_w1, expert_b1, expert_w2, expert_b2):
    h = x
    for l in range(ada_w.shape[0]):
        mod = _ada(c, ada_w[l], ada_b[l])
        h = _layer(h, mod, norm1_g[l], w_in[l], q_norm_g[l], k_norm_g[l], rel_bias_table, conv_w[l], conv_b[l],
                   dt_bias[l], a_log[l], d_skip[l], ssm_norm_g[l], w_attn_branch[l], w_ssm_branch[l],
                   gate_bias[l], w_out[l], norm2_g[l], router_w[l], router_b[l], expert_w1[l], expert_b1[l],
                   expert_w2[l], expert_b2[l])
    return h
```

```python
import functools
import math

import numpy as np
import jax
import jax.numpy as jnp
from jax import lax
from jax.experimental import pallas as pl
from jax.experimental.pallas import tpu as pltpu

F32, BF16, I32 = jnp.float32, jnp.bfloat16, jnp.int32

D = 1024
H = 16
HD = 64
ATTN_W = H * HD
SCALE = HD ** -0.5
BLK = 256
TOPK = 3
NBUCKET = 32
REL_EXACT = 16
REL_MAXD = 1024
SSM_IN = 2048
SSM_HD = 64
SSM_H = SSM_IN // SSM_HD
SSM_G = 4
SSM_J = SSM_H // SSM_G
SSM_N = 128
SSM_K = 4
L = 128
CONV_DIM = SSM_IN + 2 * SSM_G * SSM_N
NE = 32
TOPE = 4
FF = D
LIMIT = 7.0
ALPHA = 1.702
EPS = 1e-6
NEG_INF = float("-inf")

Z0, G0, X0, Q0, K0, V0, BC0 = 0, 2048, 4096, 6144, 7168, 8192, 9216
NPROJ = 10240

MOE_ROWS = 256
COMBINE_TOK = 256
VMEM_LIMIT = 56 * 1024 * 1024


def _dot(a, b):
    return jnp.dot(a, b, preferred_element_type=F32)


def _dot_nt(a, b):
    return lax.dot_general(a, b, (((1,), (1,)), ((), ())), preferred_element_type=F32)


def _cparams(*sem):
    return pltpu.CompilerParams(dimension_semantics=sem, vmem_limit_bytes=VMEM_LIMIT)


def _sigmoid(x):
    return 0.5 * jnp.tanh(0.5 * x) + 0.5


def _softplus(x):
    return jnp.maximum(x, 0.0) + jnp.log1p(jnp.exp(-jnp.abs(x)))


def _ada_body(c_ref, w_ref, b_ref, o_ref):
    c = c_ref[...]
    ca = c * _sigmoid(c)
    o_ref[...] = _dot(ca.astype(BF16), w_ref[...].astype(BF16)) + b_ref[...]


def _ada(c, w, b):
    bsz = c.shape[0]
    n = w.shape[1]
    tn = 1536
    return pl.pallas_call(
        _ada_body,
        grid=(n // tn,),
        in_specs=[pl.BlockSpec((bsz, D), lambda j: (0, 0)),
                  pl.BlockSpec((D, tn), lambda j: (0, j)),
                  pl.BlockSpec((1, tn), lambda j: (0, j))],
        out_specs=pl.BlockSpec((bsz, tn), lambda j: (0, j)),
        out_shape=jax.ShapeDtypeStruct((bsz, n), F32),
        compiler_params=_cparams("arbitrary"),
        name="ada_mod",
    )(c, w, b.reshape(1, n))


def _inproj_body(x_ref, mod_ref, g_ref, w_ref, wdtt_ref, wdt_ref, o_ref, dtt_ref, dt_ref, u_scr):
    @pl.when(pl.program_id(2) == 0)
    def _():
        rows = 256
        for r in range(x_ref.shape[1] // rows):
            sl = slice(r * rows, (r + 1) * rows)
            x = x_ref[0, sl, :]
            ms = jnp.mean(x * x, axis=-1, keepdims=True)
            xn = x * lax.rsqrt(ms + EPS) * g_ref[...]
            u = xn * (1.0 + mod_ref[0, 1:2, :]) + mod_ref[0, 0:1, :]
            ub = u.astype(BF16)
            u_scr[sl, :] = ub
            dtt_ref[0, :, sl] = _dot_nt(wdtt_ref[...], ub)
            dt_ref[0, sl, :] = _dot(ub, wdt_ref[...])

    o_ref[0] = _dot_nt(w_ref[...], u_scr[...]).astype(BF16)


def _inproj(x, mod, g1, wt, wdtt, wdt):
    bsz, s, _ = x.shape
    tm = s
    tn = 512
    return pl.pallas_call(
        _inproj_body,
        grid=(bsz, s // tm, NPROJ // tn),
        in_specs=[pl.BlockSpec((1, tm, D), lambda b, m, n: (b, m, 0)),
                  pl.BlockSpec((1, 6, D), lambda b, m, n: (b, 0, 0)),
                  pl.BlockSpec((1, D), lambda b, m, n: (0, 0)),
                  pl.BlockSpec((tn, D), lambda b, m, n: (n, 0)),
                  pl.BlockSpec((SSM_H, D), lambda b, m, n: (0, 0)),
                  pl.BlockSpec((D, SSM_H), lambda b, m, n: (0, 0))],
        out_specs=[pl.BlockSpec((1, tn, tm), lambda b, m, n: (b, n, m)),
                   pl.BlockSpec((1, SSM_H, tm), lambda b, m, n: (b, 0, m)),
                   pl.BlockSpec((1, tm, SSM_H), lambda b, m, n: (b, m, 0))],
        out_shape=[jax.ShapeDtypeStruct((bsz, NPROJ, s), BF16),
                   jax.ShapeDtypeStruct((bsz, SSM_H, s), F32),
                   jax.ShapeDtypeStruct((bsz, s, SSM_H), F32)],
        scratch_shapes=[pltpu.VMEM((tm, D), BF16)],
        compiler_params=_cparams("arbitrary", "arbitrary", "arbitrary"),
        name="in_proj",
    )(x, mod, g1, wt, wdtt, wdt)


def _t5_thresholds(max_dist):
    d = np.arange(max_dist, dtype=np.int64)
    df = np.maximum(d, 1).astype(np.float32)
    large = REL_EXACT + (np.log(df / np.float32(REL_EXACT)) / np.float32(math.log(REL_MAXD / REL_EXACT))
                         * np.float32(NBUCKET - REL_EXACT)).astype(np.int32)
    bucket = np.where(d < REL_EXACT, d, np.minimum(large, NBUCKET - 1))
    return [int(np.argmax(bucket >= b)) if np.any(bucket >= b) else int(max_dist) for b in range(NBUCKET)]


def _bias_body(tab_ref, o_ref, *, thr):
    h = pl.program_id(0)
    delta = pl.program_id(1)
    ki = lax.broadcasted_iota(I32, (BLK, BLK), 0)
    qi = lax.broadcasted_iota(I32, (BLK, BLK), 1)
    d = jnp.maximum(delta * BLK + qi - ki, 0)
    val = jnp.full((BLK, BLK), tab_ref[h, 0], F32)
    for b in range(1, NBUCKET):
        val = jnp.where(d >= thr[b], tab_ref[h, b], val)
    o_ref[0, 0] = jnp.where(jnp.logical_or(delta > 0, qi >= ki), val, NEG_INF)


def _bias_tiles(table_t, nb):
    thr = _t5_thresholds(nb * BLK + BLK)
    return pl.pallas_call(
        functools.partial(_bias_body, thr=thr),
        grid=(H, nb),
        in_specs=[pl.BlockSpec(memory_space=pltpu.SMEM)],
        out_specs=pl.BlockSpec((1, 1, BLK, BLK), lambda h, d: (h, d, 0, 0)),
        out_shape=jax.ShapeDtypeStruct((H, nb, BLK, BLK), F32),
        compiler_params=_cparams("arbitrary", "arbitrary"),
        name="t5_bias_tiles",
    )(table_t)


def _attn_body(q_ref, k_ref, v_ref, bias_ref, qg_ref, kg_ref, o_ref, k_scr, sc_scr, p_scr, *, s):
    nb = s // BLK

    def normed(ref, g_ref, hh):
        t = ref[0, hh * HD:(hh + 1) * HD, :].astype(F32)
        ms = jnp.mean(t * t, axis=0, keepdims=True)
        return t * lax.rsqrt(ms + EPS) * g_ref[...]

    qn = [normed(q_ref, qg_ref, hh) for hh in range(2)]
    kn = [normed(k_ref, kg_ref, hh) for hh in range(2)]
    kpair = jnp.concatenate(kn, axis=0)
    for j in range(nb):
        k_scr[j] = kpair[:, j * BLK:(j + 1) * BLK].T.astype(BF16)
    zeros_q = jnp.zeros((HD, BLK), F32)

    for hh in range(2):
        gate = []
        for j in range(nb):
            kmean = jnp.mean(kn[hh][:, j * BLK:(j + 1) * BLK], axis=1, keepdims=True)
            gate.append(jnp.sum(qn[hh] * kmean, axis=0, keepdims=True))
        qs = qn[hh] * SCALE
        for i in range(nb):
            sel = [None] * i
            if i > TOPK:
                rows = [g[:, i * BLK:(i + 1) * BLK] for g in gate[:i]]
                for j in range(i):
                    rank = jnp.zeros((1, BLK), F32)
                    for jj in range(i):
                        if jj != j:
                            beats = (rows[jj] >= rows[j]) if jj < j else (rows[jj] > rows[j])
                            rank = rank + jnp.where(beats, 1.0, 0.0)
                    sel[j] = jnp.where(rank < TOPK, 0.0, NEG_INF)
            qblk = qs[:, i * BLK:(i + 1) * BLK]
            qpad = jnp.concatenate([qblk, zeros_q] if hh == 0 else [zeros_q, qblk], axis=0).astype(BF16)
            m = None
            for j in range(i + 1):
                sc = _dot(k_scr[j], qpad) + bias_ref[hh, i - j]
                if j < i and sel[j] is not None:
                    sc = sc + sel[j]
                sc_scr[j] = sc
                mj = jnp.max(sc, axis=0, keepdims=True)
                m = mj if m is None else jnp.maximum(m, mj)
            l = jnp.zeros((1, BLK), F32)
            for j in range(i + 1):
                p = jnp.exp(sc_scr[j] - m)
                l = l + jnp.sum(p, axis=0, keepdims=True)
                p_scr[j * BLK:(j + 1) * BLK, :] = p.astype(BF16)
            acc = _dot(v_ref[0, hh * HD:(hh + 1) * HD, 0:(i + 1) * BLK], p_scr[0:(i + 1) * BLK, :])
            o_ref[0, hh * HD:(hh + 1) * HD, i * BLK:(i + 1) * BLK] = (acc / l).astype(BF16)


def _attention(projt, bias, qg, kg):
    bsz, _, s = projt.shape
    nb = s // BLK
    rows = 2 * HD
    qb, kb, vb = Q0 // rows, K0 // rows, V0 // rows
    return pl.pallas_call(
        functools.partial(_attn_body, s=s),
        grid=(H // 2, bsz),
        in_specs=[pl.BlockSpec((1, rows, s), lambda p, b: (b, qb + p, 0)),
                  pl.BlockSpec((1, rows, s), lambda p, b: (b, kb + p, 0)),
                  pl.BlockSpec((1, rows, s), lambda p, b: (b, vb + p, 0)),
                  pl.BlockSpec((2, nb, BLK, BLK), lambda p, b: (p, 0, 0, 0)),
                  pl.BlockSpec((HD, 1), lambda p, b: (0, 0)),
                  pl.BlockSpec((HD, 1), lambda p, b: (0, 0))],
        out_specs=pl.BlockSpec((1, rows, s), lambda p, b: (b, p, 0)),
        out_shape=jax.ShapeDtypeStruct((bsz, ATTN_W, s), BF16),
        scratch_shapes=[pltpu.VMEM((nb, BLK, rows), BF16),
                        pltpu.VMEM((nb, BLK, BLK), F32),
                        pltpu.VMEM((s, BLK), BF16)],
        compiler_params=_cparams("arbitrary", "arbitrary"),
        name="moba_attention",
    )(projt, projt, projt, bias, qg, kg)


def _split3(x):
    hi = x.astype(BF16)
    r1 = x - hi.astype(F32)
    mid = r1.astype(BF16)
    lo = (r1 - mid.astype(F32)).astype(BF16)
    return hi, mid, lo


def _ssd_body(z_ref, xs_ref, bc_ref, dtt_ref, dt_ref, cw_ref, cb_ref, dtbc_ref, dtbr_ref, alc_ref, alr_ref,
              dskip_ref, ng_ref, o_ref, prevx_scr, state_scr):
    @pl.when(pl.program_id(1) == 0)
    def _():
        prevx_scr[...] = jnp.zeros_like(prevx_scr)
        state_scr[...] = jnp.zeros_like(state_scr)

    cur = jnp.concatenate([xs_ref[0], bc_ref[0]], axis=0).astype(F32)
    prev = prevx_scr[...]
    lane = lax.broadcasted_iota(I32, (1, L), 1)
    acc = cb_ref[...] + cw_ref[SSM_K - 1] * cur
    for sft in range(1, SSM_K):
        shifted = pltpu.roll(jnp.where(lane >= L - sft, prev, cur), sft, 1)
        acc = acc + cw_ref[SSM_K - 1 - sft] * shifted
    prevx_scr[...] = cur
    xbc = acc * _sigmoid(acc)

    dtt = _softplus(dtt_ref[0] + dtbc_ref[...])
    dts = _softplus(dt_ref[0] + dtbr_ref[...])
    adtt = dtt * (-jnp.exp(alc_ref[...]))
    adts = dts * (-jnp.exp(alr_ref[...]))
    r_i = lax.broadcasted_iota(I32, (L, L), 0)
    c_i = lax.broadcasted_iota(I32, (L, L), 1)
    upper = jnp.where(r_i <= c_i, 1.0, 0.0).astype(BF16)
    lower = jnp.where(r_i >= c_i, 1.0, 0.0).astype(BF16)
    acst = sum(_dot(part, upper) for part in _split3(adtt))
    acs = sum(_dot(lower, part) for part in _split3(adts))
    causal = c_i >= r_i
    last = acst[:, L - 1:L]
    dte = jnp.exp(last - acst)
    cdec = jnp.broadcast_to(jnp.exp(last), (SSM_H, L))
    ein = jnp.exp(acst)

    for g in range(SSM_G):
        bt = xbc[SSM_IN + g * SSM_N:SSM_IN + (g + 1) * SSM_N]
        ct = xbc[SSM_IN + SSM_G * SSM_N + g * SSM_N:SSM_IN + SSM_G * SSM_N + (g + 1) * SSM_N]
        ctb = ct.astype(BF16)
        bmb = bt.T.astype(BF16)
        cbt = _dot(bmb, ctb)
        rows = slice(g * SSM_J * SSM_HD, (g + 1) * SSM_J * SSM_HD)
        xg = xbc[rows]
        state = state_scr[rows]
        yoff = _dot(state.astype(BF16), ctb)
        xdt, sin, cd, ys = [], [], [], []
        for j in range(SSM_J):
            h = g * SSM_J + j
            xj = xg[j * SSM_HD:(j + 1) * SSM_HD]
            xdj = xj * dtt[h:h + 1]
            xdt.append(xdj)
            sin.append(xdj * dte[h:h + 1])
            cd.append(jnp.broadcast_to(cdec[h:h + 1], (SSM_HD, SSM_N)))
        new_state = _dot(jnp.concatenate(sin, axis=0).astype(BF16), bmb)
        state_scr[rows] = state * jnp.concatenate(cd, axis=0) + new_state
        for j in range(SSM_J):
            h = g * SSM_J + j
            seg = acst[h:h + 1] - acs[:, h:h + 1]
            mt = cbt * jnp.exp(jnp.where(causal, seg, NEG_INF))
            ydiag = _dot(xdt[j].astype(BF16), mt.astype(BF16))
            ys.append(ydiag + yoff[j * SSM_HD:(j + 1) * SSM_HD] * ein[h:h + 1])
        y = jnp.concatenate(ys, axis=0) + dskip_ref[rows] * xg
        zg = z_ref[0, rows].astype(F32)
        y = y * (zg * _sigmoid(zg))
        ms = jnp.mean(y * y, axis=0, keepdims=True)
        o_ref[0, rows] = (y * lax.rsqrt(ms + EPS) * ng_ref[rows]).astype(BF16)


def _ssd(projt, dtt, dts, cw, cb, dtb, alog, dskip, ng):
    bsz, _, s = projt.shape
    nc = s // L
    full = lambda shape: pl.BlockSpec(shape, lambda b, c: (0,) * len(shape))
    return pl.pallas_call(
        _ssd_body,
        grid=(bsz, nc),
        in_specs=[pl.BlockSpec((1, SSM_IN, L), lambda b, c: (b, Z0 // SSM_IN, c)),
                  pl.BlockSpec((1, SSM_IN, L), lambda b, c: (b, X0 // SSM_IN, c)),
                  pl.BlockSpec((1, 2 * SSM_G * SSM_N, L), lambda b, c: (b, BC0 // (2 * SSM_G * SSM_N), c)),
                  pl.BlockSpec((1, SSM_H, L), lambda b, c: (b, 0, c)),
                  pl.BlockSpec((1, L, SSM_H), lambda b, c: (b, c, 0)),
                  full((SSM_K, CONV_DIM, L)), full((CONV_DIM, L)),
                  full((SSM_H, 1)), full((1, SSM_H)), full((SSM_H, 1)), full((1, SSM_H)),
                  full((SSM_IN, L)), full((SSM_IN, L))],
        out_specs=pl.BlockSpec((1, SSM_IN, L), lambda b, c: (b, 0, c)),
        out_shape=jax.ShapeDtypeStruct((bsz, SSM_IN, s), BF16),
        scratch_shapes=[pltpu.VMEM((CONV_DIM, L), F32), pltpu.VMEM((SSM_IN, SSM_N), F32)],
        compiler_params=_cparams("arbitrary", "arbitrary"),
        name="ssd_branch",
    )(projt, projt, projt, dtt, dts, cw, cb, dtb.reshape(SSM_H, 1), dtb.reshape(1, SSM_H),
      alog.reshape(SSM_H, 1), alog.reshape(1, SSM_H), dskip, ng)


def _merge_body(attn_ref, ssm_ref, gate_ref, x_ref, mod_ref, gb_ref, wat_ref, wst_ref, wo_ref, g2_ref,
                rwt_ref, rb_ref, x1_ref, u2_ref, lg_ref):
    a = _dot(wat_ref[...], attn_ref[0])
    sm = _dot(wst_ref[...], ssm_ref[0])
    gate = _sigmoid(gate_ref[0].astype(F32) + gb_ref[...])
    merged_t = gate[:D] * a + gate[D:] * sm
    merged = merged_t.T.astype(BF16)
    y = _dot(merged, wo_ref[...])
    x1 = x_ref[0] + mod_ref[0, 2:3, :] * y
    x1_ref[0] = x1
    ms = jnp.mean(x1 * x1, axis=-1, keepdims=True)
    u2 = (x1 * lax.rsqrt(ms + EPS) * g2_ref[...]) * (1.0 + mod_ref[0, 4:5, :]) + mod_ref[0, 3:4, :]
    u2_ref[0] = u2
    lg_ref[...] = _dot_nt(rwt_ref[...], u2.astype(BF16)) + rb_ref[...]


def _merge(attnt, ssmt, projt, x, mod, gb, wat, wst, wo, g2, rwt, rb):
    bsz, s, _ = x.shape
    tm = 512
    nm = s // tm
    full = lambda shape: pl.BlockSpec(shape, lambda b, m: (0,) * len(shape))
    return pl.pallas_call(
        _merge_body,
        grid=(bsz, nm),
        in_specs=[pl.BlockSpec((1, ATTN_W, tm), lambda b, m: (b, 0, m)),
                  pl.BlockSpec((1, SSM_IN, tm), lambda b, m: (b, 0, m)),
                  pl.BlockSpec((1, 2 * D, tm), lambda b, m: (b, G0 // (2 * D), m)),
                  pl.BlockSpec((1, tm, D), lambda b, m: (b, m, 0)),
                  pl.BlockSpec((1, 6, D), lambda b, m: (b, 0, 0)),
                  full((2 * D, 1)), full((D, ATTN_W)), full((D, SSM_IN)), full((D, D)), full((1, D)),
                  full((NE, D)), full((NE, 1))],
        out_specs=[pl.BlockSpec((1, tm, D), lambda b, m: (b, m, 0)),
                   pl.BlockSpec((1, tm, D), lambda b, m: (b, m, 0)),
                   pl.BlockSpec((NE, tm), lambda b, m: (0, b * nm + m))],
        out_shape=[jax.ShapeDtypeStruct((bsz, s, D), F32),
                   jax.ShapeDtypeStruct((bsz, s, D), F32),
                   jax.ShapeDtypeStruct((NE, bsz * s), F32)],
        compiler_params=_cparams("arbitrary", "arbitrary"),
        name="merge_outproj",
    )(attnt, ssmt, projt, x, mod, gb, wat, wst, wo, g2, rwt, rb)


def _route_body(lg_ref, e_ref, w_ref, r_ref, cnt_ref, carry_scr, *, tm):
    @pl.when(pl.program_id(0) == 0)
    def _():
        carry_scr[...] = jnp.zeros_like(carry_scr)

    work = lg_ref[...]
    eidx = lax.broadcasted_iota(I32, (NE, tm), 0).astype(F32)
    vals, idxs, hots = [], [], []
    for _ in range(TOPE):
        m = jnp.max(work, axis=0, keepdims=True)
        idx = jnp.min(jnp.where(work == m, eidx, float(NE)), axis=0, keepdims=True)
        hot = eidx == idx
        work = jnp.where(hot, NEG_INF, work)
        vals.append(m)
        idxs.append(idx)
        hots.append(hot)
    ex = [jnp.exp(v - vals[0]) for v in vals]
    den = ex[0] + ex[1] + ex[2] + ex[3]
    multi = jnp.zeros((NE, tm), F32)
    for hot in hots:
        multi = multi + jnp.where(hot, 1.0, 0.0)
    r_i = lax.broadcasted_iota(I32, (tm, tm), 0)
    c_i = lax.broadcasted_iota(I32, (tm, tm), 1)
    strict = jnp.where(r_i < c_i, 1.0, 0.0).astype(BF16)
    before = _dot(multi.astype(BF16), strict) + carry_scr[:, 0:1]
    for k in range(TOPE):
        e_ref[k:k + 1, :] = idxs[k].astype(I32)
        w_ref[k:k + 1, :] = ex[k] / den
        r_ref[k:k + 1, :] = jnp.sum(jnp.where(hots[k], before, 0.0), axis=0, keepdims=True).astype(I32)
    total = carry_scr[...] + jnp.sum(multi, axis=1, keepdims=True)
    carry_scr[...] = total
    cnt_ref[...] = total


def _route(logits_t):
    t = logits_t.shape[1]
    tm = 512
    return pl.pallas_call(
        functools.partial(_route_body, tm=tm),
        grid=(t // tm,),
        in_specs=[pl.BlockSpec((NE, tm), lambda i: (0, i))],
        out_specs=[pl.BlockSpec((TOPE, tm), lambda i: (0, i)),
                   pl.BlockSpec((TOPE, tm), lambda i: (0, i)),
                   pl.BlockSpec((TOPE, tm), lambda i: (0, i)),
                   pl.BlockSpec((NE, 128), lambda i: (0, 0))],
        out_shape=[jax.ShapeDtypeStruct((TOPE, t), I32),
                   jax.ShapeDtypeStruct((TOPE, t), F32),
                   jax.ShapeDtypeStruct((TOPE, t), I32),
                   jax.ShapeDtypeStruct((NE, 128), F32)],
        scratch_shapes=[pltpu.VMEM((NE, 128), F32)],
        compiler_params=_cparams("arbitrary"),
        name="route_top4",
    )(logits_t)


def _row_copy(src_hbm, src_row, dst, dst_row, sem):
    return pltpu.make_async_copy(src_hbm.at[pl.ds(src_row, 1)], dst.at[pl.ds(dst_row, 1)], sem)


def _dispatch_body(pe_ref, nu_ref, dest_ref, u_ref, o_hbm, zero_scr, sem, *, nblk):
    def fill(blk):
        return pltpu.make_async_copy(zero_scr, o_hbm.at[pl.ds(blk * MOE_ROWS, MOE_ROWS)], sem)

    @pl.when(pl.program_id(0) == 0)
    def _():
        zero_scr[...] = jnp.zeros_like(zero_scr)
        pad_blocks = []
        for e in range(NE):
            prev_end = pe_ref[e - 1] if e > 0 else 0
            pad_blocks.append((pe_ref[e] > prev_end, pe_ref[e] // MOE_ROWS - 1))
        for b in range(NE):
            pad_blocks.append((nblk - 1 - b >= nu_ref[0], nblk - 1 - b))
        for cond, blk in pad_blocks:
            @pl.when(cond)
            def _(blk=blk):
                fill(blk).start()
        for cond, blk in pad_blocks:
            @pl.when(cond)
            def _(blk=blk):
                fill(blk).wait()

    def start(r, c):
        for k in range(TOPE):
            _row_copy(u_ref, r, o_hbm, dest_ref[0, k, r], sem).start(priority=k % 2)
        return c

    def wait(r, c):
        for k in range(TOPE):
            _row_copy(u_ref, 0, o_hbm, 0, sem).wait()
        return c

    lax.fori_loop(0, COMBINE_TOK, start, 0)
    lax.fori_loop(0, COMBINE_TOK, wait, 0)


def _dispatch(u2, dest_blocks, p_end, n_used, cap):
    t = u2.shape[0]
    tm = COMBINE_TOK
    nblk = cap // MOE_ROWS
    grid_spec = pltpu.PrefetchScalarGridSpec(
        num_scalar_prefetch=2,
        grid=(t // tm,),
        in_specs=[pl.BlockSpec((1, TOPE, tm), lambda i, pe, nu: (i, 0, 0), memory_space=pltpu.SMEM),
                  pl.BlockSpec((tm, D), lambda i, pe, nu: (i, 0))],
        out_specs=pl.BlockSpec(memory_space=pl.ANY),
        scratch_shapes=[pltpu.VMEM((MOE_ROWS, D), F32), pltpu.SemaphoreType.DMA(())],
    )
    return pl.pallas_call(
        functools.partial(_dispatch_body, nblk=nblk),
        grid_spec=grid_spec,
        out_shape=jax.ShapeDtypeStruct((cap, D), F32),
        compiler_params=_cparams("arbitrary"),
        name="moe_dispatch",
    )(p_end, n_used, dest_blocks, u2)


def _expert_body(be_ref, nu_ref, x_ref, w1_ref, b1_ref, w2_ref, b2_ref, o_ref, w1_scr, w2_scr):
    i = pl.program_id(0)
    changed = jnp.logical_or(i == 0, be_ref[i] != be_ref[jnp.maximum(i - 1, 0)])

    @pl.when(changed)
    def _():
        w1_scr[...] = w1_ref[0].astype(BF16)
        w2_scr[...] = w2_ref[0].astype(BF16)

    @pl.when(i < nu_ref[0])
    def _():
        h = _dot(x_ref[...].astype(BF16), w1_scr[...]) + b1_ref[0]
        glu = jnp.minimum(h[:, :FF], LIMIT)
        lin = jnp.clip(h[:, FF:], -LIMIT, LIMIT)
        act = glu * _sigmoid(ALPHA * glu) * (lin + 1.0)
        o_ref[...] = _dot(act.astype(BF16), w2_scr[...]) + b2_ref[0]

    @pl.when(i >= nu_ref[0])
    def _():
        o_ref[...] = jnp.zeros_like(o_ref)


def _experts(xs, block_e, n_used, w1, b1, w2, b2):
    cap = xs.shape[0]
    nblk = cap // MOE_ROWS
    grid_spec = pltpu.PrefetchScalarGridSpec(
        num_scalar_prefetch=2,
        grid=(nblk,),
        in_specs=[pl.BlockSpec((MOE_ROWS, D), lambda i, be, nu: (i, 0)),
                  pl.BlockSpec((1, D, 2 * FF), lambda i, be, nu: (be[i], 0, 0)),
                  pl.BlockSpec((1, 1, 2 * FF), lambda i, be, nu: (be[i], 0, 0)),
                  pl.BlockSpec((1, FF, D), lambda i, be, nu: (be[i], 0, 0)),
                  pl.BlockSpec((1, 1, D), lambda i, be, nu: (be[i], 0, 0))],
        out_specs=pl.BlockSpec((MOE_ROWS, D), lambda i, be, nu: (i, 0)),
        scratch_shapes=[pltpu.VMEM((D, 2 * FF), BF16), pltpu.VMEM((FF, D), BF16)],
    )
    return pl.pallas_call(
        _expert_body,
        grid_spec=grid_spec,
        out_shape=jax.ShapeDtypeStruct((cap, D), F32),
        compiler_params=_cparams("arbitrary"),
        name="moe_experts",
    )(block_e, n_used, xs, w1, b1.reshape(NE, 1, 2 * FF), w2, b2.reshape(NE, 1, D))


def _combine_body(dest_ref, ys_hbm, w_ref, x1_ref, g2_ref, o_ref, buf, sem):
    def start(r, c):
        for k in range(TOPE):
            _row_copy(ys_hbm, dest_ref[0, k, r], buf.at[k], r, sem).start(priority=k % 2)
        return c

    def wait(r, c):
        for k in range(TOPE):
            _row_copy(ys_hbm, 0, buf.at[k], 0, sem).wait()
        return c

    lax.fori_loop(0, COMBINE_TOK, start, 0)
    lax.fori_loop(0, COMBINE_TOK, wait, 0)
    w = w_ref[...]
    ffn = buf[0] * w[:, 0:1]
    for k in range(1, TOPE):
        ffn = ffn + buf[k] * w[:, k:k + 1]
    o_ref[0] = x1_ref[0] + g2_ref[0] * ffn


def _combine(ys, dest_blocks, top_w, x1, gate2):
    bsz, s, _ = x1.shape
    tm = COMBINE_TOK
    nm = s // tm
    return pl.pallas_call(
        _combine_body,
        grid=(bsz, nm),
        in_specs=[pl.BlockSpec((1, TOPE, tm), lambda b, m: (b * nm + m, 0, 0), memory_space=pltpu.SMEM),
                  pl.BlockSpec(memory_space=pl.ANY),
                  pl.BlockSpec((tm, TOPE), lambda b, m: (b * nm + m, 0)),
                  pl.BlockSpec((1, tm, D), lambda b, m: (b, m, 0)),
                  pl.BlockSpec((1, 1, D), lambda b, m: (b, 0, 0))],
        out_specs=pl.BlockSpec((1, tm, D), lambda b, m: (b, m, 0)),
        out_shape=jax.ShapeDtypeStruct((bsz, s, D), F32),
        scratch_shapes=[pltpu.VMEM((TOPE, tm, D), F32), pltpu.SemaphoreType.DMA(())],
        compiler_params=_cparams("arbitrary", "arbitrary"),
        name="moe_combine",
    )(dest_blocks, ys, top_w.T, x1, gate2)


def _layer(x, mod, norm1_g, w_in, q_norm_g, k_norm_g, rel_bias_table, conv_w, conv_b, dt_bias, a_log, d_skip,
           ssm_norm_g, w_attn_branch, w_ssm_branch, gate_bias, w_out, norm2_g, router_w, router_b,
           expert_w1, expert_b1, expert_w2, expert_b2):
    bsz, s, _ = x.shape
    t = bsz * s
    assert s % BLK == 0 and s % 512 == 0
    mod3 = mod.reshape(bsz, 6, D)

    c0 = 3 * ATTN_W
    c_x, c_b, c_dt, c_g = c0 + SSM_IN, c0 + SSM_IN + SSM_IN, c0 + SSM_IN + CONV_DIM, c0 + SSM_IN + CONV_DIM + SSM_H
    w_rows = jnp.concatenate([w_in[:, c0:c_x], w_in[:, c_g:], w_in[:, c_x:c_b], w_in[:, :c0], w_in[:, c_b:c_dt]],
                             axis=1)
    wt = w_rows.T.astype(BF16)
    wdt = w_in[:, c_dt:c_g].astype(BF16)
    projt, dtt, dts = _inproj(x, mod3, norm1_g.reshape(1, D), wt, wdt.T, wdt)

    bias = _bias_tiles(rel_bias_table.T.astype(F32), s // BLK)
    attnt = _attention(projt, bias, q_norm_g.reshape(HD, 1), k_norm_g.reshape(HD, 1))

    lanes = lambda v: jnp.broadcast_to(v[..., None], v.shape + (L,))
    ssmt = _ssd(projt, dtt, dts, lanes(conv_w), lanes(conv_b), dt_bias, a_log,
                lanes(jnp.repeat(d_skip, SSM_HD)), lanes(ssm_norm_g))

    x1, u2, logits_t = _merge(attnt, ssmt, projt, x, mod3, gate_bias.reshape(2 * D, 1),
                              w_attn_branch.T.astype(BF16), w_ssm_branch.T.astype(BF16), w_out.astype(BF16),
                              norm2_g.reshape(1, D), router_w.T.astype(BF16), router_b.reshape(NE, 1))

    top_e, top_w, rank, cnt = _route(logits_t)
    counts = cnt[:, 0].astype(I32)
    padded = (counts + MOE_ROWS - 1) // MOE_ROWS * MOE_ROWS
    p_end = jnp.cumsum(padded)
    p_start = p_end - padded
    experts = jnp.arange(NE, dtype=I32)
    dest = rank + jnp.sum(jnp.where(top_e[..., None] == experts, p_start, 0), axis=-1)
    dest_blocks = dest.reshape(TOPE, t // COMBINE_TOK, COMBINE_TOK).transpose(1, 0, 2)
    cap = TOPE * t + NE * MOE_ROWS
    nblk = cap // MOE_ROWS
    blk_start = jnp.arange(nblk, dtype=I32) * MOE_ROWS
    block_e = jnp.minimum(jnp.sum((p_end[None, :] <= blk_start[:, None]).astype(I32), axis=1), NE - 1)
    n_used = (p_end[-1] // MOE_ROWS).astype(I32).reshape(1)

    xs = _dispatch(u2.reshape(t, D), dest_blocks, p_end.astype(I32), n_used, cap)
    ys = _experts(xs, block_e, n_used, expert_w1, expert_b1, expert_w2, expert_b2)
    return _combine(ys, dest_blocks, top_w, x1, mod3[:, 5:6, :])


def kernel(x, c, ada_w, ada_b, norm1_g, w_in, q_norm_g, k_norm_g, rel_bias_table, conv_w, conv_b, dt_bias, a_log,
           d_skip, ssm_norm_g, w_attn_branch, w_ssm_branch, gate_bias, w_out, norm2_g, router_w, router_b,
           expert_w1, expert_b1, expert_w2, expert_b2):
    h = x
    for l in range(ada_w.shape[0]):
        mod = _ada(c, ada_w[l], ada_b[l])
        h = _layer(h, mod, norm1_g[l], w_in[l], q_norm_g[l], k_norm_g[l], rel_bias_table, conv_w[l], conv_b[l],
                   dt_bias[l], a_log[l], d_skip[l], ssm_norm_g[l], w_attn_branch[l], w_ssm_branch[l],
                   gate_bias[l], w_out[l], norm2_g[l], router_w[l], router_b[l], expert_w1[l], expert_b1[l],
                   expert_w2[l], expert_b2[l])
    return h
```

```python
import functools
import math

import numpy as np
import jax
import jax.numpy as jnp
from jax import lax
from jax.experimental import pallas as pl
from jax.experimental.pallas import tpu as pltpu

F32, BF16, I32 = jnp.float32, jnp.bfloat16, jnp.int32

D = 1024
H = 16
HD = 64
ATTN_W = H * HD
SCALE = HD ** -0.5
BLK = 256
TOPK = 3
NBUCKET = 32
REL_EXACT = 16
REL_MAXD = 1024
SSM_IN = 2048
SSM_HD = 64
SSM_H = SSM_IN // SSM_HD
SSM_G = 4
SSM_J = SSM_H // SSM_G
SSM_N = 128
SSM_K = 4
L = 128
CONV_DIM = SSM_IN + 2 * SSM_G * SSM_N
NE = 32
TOPE = 4
FF = D
LIMIT = 7.0
ALPHA = 1.702
EPS = 1e-6
NEG_INF = float("-inf")

Z0, G0, X0, Q0, K0, V0, BC0 = 0, 2048, 4096, 6144, 7168, 8192, 9216
NPROJ = 10240

VA_ROWS = HD + 16
MOE_ROWS = 512
COMBINE_TOK = 256
VMEM_LIMIT = 56 * 1024 * 1024


def _dot(a, b):
    return jnp.dot(a, b, preferred_element_type=F32)


def _dot_nt(a, b):
    return lax.dot_general(a, b, (((1,), (1,)), ((), ())), preferred_element_type=F32)


def _cparams(*sem):
    return pltpu.CompilerParams(dimension_semantics=sem, vmem_limit_bytes=VMEM_LIMIT)


def _sigmoid(x):
    return 0.5 * jnp.tanh(0.5 * x) + 0.5


def _softplus(x):
    return jnp.maximum(x, 0.0) + jnp.log1p(jnp.exp(-jnp.abs(x)))


def _ada_body(c_ref, w_ref, b_ref, o_ref):
    c = c_ref[...]
    ca = c * _sigmoid(c)
    o_ref[...] = _dot(ca.astype(BF16), w_ref[...].astype(BF16)) + b_ref[...]


def _ada(c, w, b):
    bsz = c.shape[0]
    n = w.shape[1]
    tn = 1536
    return pl.pallas_call(
        _ada_body,
        grid=(n // tn,),
        in_specs=[pl.BlockSpec((bsz, D), lambda j: (0, 0)),
                  pl.BlockSpec((D, tn), lambda j: (0, j)),
                  pl.BlockSpec((1, tn), lambda j: (0, j))],
        out_specs=pl.BlockSpec((bsz, tn), lambda j: (0, j)),
        out_shape=jax.ShapeDtypeStruct((bsz, n), F32),
        compiler_params=_cparams("arbitrary"),
        name="ada_mod",
    )(c, w, b.reshape(1, n))


def _inproj_body(x_ref, mod_ref, g_ref, w_ref, wdtt_ref, wdt_ref, o_ref, dtt_ref, dt_ref, u_scr):
    @pl.when(pl.program_id(2) == 0)
    def _():
        rows = 256
        for r in range(x_ref.shape[1] // rows):
            sl = slice(r * rows, (r + 1) * rows)
            x = x_ref[0, sl, :]
            ms = jnp.mean(x * x, axis=-1, keepdims=True)
            xn = x * lax.rsqrt(ms + EPS) * g_ref[...]
            u = xn * (1.0 + mod_ref[0, 1:2, :]) + mod_ref[0, 0:1, :]
            ub = u.astype(BF16)
            u_scr[sl, :] = ub
            dtt_ref[0, :, sl] = _dot_nt(wdtt_ref[...], ub)
            dt_ref[0, sl, :] = _dot(ub, wdt_ref[...])

    o_ref[0] = _dot_nt(w_ref[...], u_scr[...]).astype(BF16)


def _inproj(x, mod, g1, wt, wdtt, wdt):
    bsz, s, _ = x.shape
    tm = s
    tn = 1024
    return pl.pallas_call(
        _inproj_body,
        grid=(bsz, s // tm, NPROJ // tn),
        in_specs=[pl.BlockSpec((1, tm, D), lambda b, m, n: (b, m, 0)),
                  pl.BlockSpec((1, 6, D), lambda b, m, n: (b, 0, 0)),
                  pl.BlockSpec((1, D), lambda b, m, n: (0, 0)),
                  pl.BlockSpec((tn, D), lambda b, m, n: (n, 0)),
                  pl.BlockSpec((SSM_H, D), lambda b, m, n: (0, 0)),
                  pl.BlockSpec((D, SSM_H), lambda b, m, n: (0, 0))],
        out_specs=[pl.BlockSpec((1, tn, tm), lambda b, m, n: (b, n, m)),
                   pl.BlockSpec((1, SSM_H, tm), lambda b, m, n: (b, 0, m)),
                   pl.BlockSpec((1, tm, SSM_H), lambda b, m, n: (b, m, 0))],
        out_shape=[jax.ShapeDtypeStruct((bsz, NPROJ, s), BF16),
                   jax.ShapeDtypeStruct((bsz, SSM_H, s), F32),
                   jax.ShapeDtypeStruct((bsz, s, SSM_H), F32)],
        scratch_shapes=[pltpu.VMEM((tm, D), BF16)],
        compiler_params=_cparams("arbitrary", "arbitrary", "arbitrary"),
        name="in_proj",
    )(x, mod, g1, wt, wdtt, wdt)


def _t5_thresholds(max_dist):
    d = np.arange(max_dist, dtype=np.int64)
    df = np.maximum(d, 1).astype(np.float32)
    large = REL_EXACT + (np.log(df / np.float32(REL_EXACT)) / np.float32(math.log(REL_MAXD / REL_EXACT))
                         * np.float32(NBUCKET - REL_EXACT)).astype(np.int32)
    bucket = np.where(d < REL_EXACT, d, np.minimum(large, NBUCKET - 1))
    return [int(np.argmax(bucket >= b)) if np.any(bucket >= b) else int(max_dist) for b in range(NBUCKET)]


def _bias_body(tab_ref, o_ref, *, thr):
    h = pl.program_id(0)
    delta = pl.program_id(1)
    ki = lax.broadcasted_iota(I32, (BLK, BLK), 0)
    qi = lax.broadcasted_iota(I32, (BLK, BLK), 1)
    d = jnp.maximum(delta * BLK + qi - ki, 0)
    val = jnp.full((BLK, BLK), tab_ref[h, 0], F32)
    for b in range(1, NBUCKET):
        val = jnp.where(d >= thr[b], tab_ref[h, b], val)
    o_ref[0, 0] = jnp.where(jnp.logical_or(delta > 0, qi >= ki), val, NEG_INF)


def _bias_tiles(table_t, nb):
    thr = _t5_thresholds(nb * BLK + BLK)
    return pl.pallas_call(
        functools.partial(_bias_body, thr=thr),
        grid=(H, nb),
        in_specs=[pl.BlockSpec(memory_space=pltpu.SMEM)],
        out_specs=pl.BlockSpec((1, 1, BLK, BLK), lambda h, d: (h, d, 0, 0)),
        out_shape=jax.ShapeDtypeStruct((H, nb, BLK, BLK), F32),
        compiler_params=_cparams("arbitrary", "arbitrary"),
        name="t5_bias_tiles",
    )(table_t)


def _attn_body(q_ref, k_ref, v_ref, bias_ref, qg_ref, kg_ref, o_ref, k_scr, sc_scr, p_scr, va_scr, *, s):
    nb = s // BLK

    def normed(ref, g_ref, hh):
        t = ref[0, hh * HD:(hh + 1) * HD, :].astype(F32)
        ms = jnp.mean(t * t, axis=0, keepdims=True)
        return t * lax.rsqrt(ms + EPS) * g_ref[...]

    qn = [normed(q_ref, qg_ref, hh) for hh in range(2)]
    kn = [normed(k_ref, kg_ref, hh) for hh in range(2)]
    kpair = jnp.concatenate(kn, axis=0)
    for j in range(nb):
        k_scr[j] = kpair[:, j * BLK:(j + 1) * BLK].T.astype(BF16)
    zeros_q = jnp.zeros((HD, BLK), F32)
    row = lax.broadcasted_iota(I32, (VA_ROWS - HD, s), 0)
    ones_row = jnp.where(row == 0, 1.0, 0.0).astype(BF16)
    for hh in range(2):
        va_scr[hh, 0:HD, :] = v_ref[0, hh * HD:(hh + 1) * HD, :]
        va_scr[hh, HD:VA_ROWS, :] = ones_row

    for hh in range(2):
        gate = []
        for j in range(nb):
            kmean = jnp.mean(kn[hh][:, j * BLK:(j + 1) * BLK], axis=1, keepdims=True)
            gate.append(jnp.sum(qn[hh] * kmean, axis=0, keepdims=True))
        qs = qn[hh] * SCALE
        for i in range(nb):
            sel = [None] * i
            if i > TOPK:
                rows = [g[:, i * BLK:(i + 1) * BLK] for g in gate[:i]]
                for j in range(i):
                    rank = jnp.zeros((1, BLK), F32)
                    for jj in range(i):
                        if jj != j:
                            beats = (rows[jj] >= rows[j]) if jj < j else (rows[jj] > rows[j])
                            rank = rank + jnp.where(beats, 1.0, 0.0)
                    sel[j] = jnp.where(rank < TOPK, 0.0, NEG_INF)
            qblk = qs[:, i * BLK:(i + 1) * BLK]
            qpad = jnp.concatenate([qblk, zeros_q] if hh == 0 else [zeros_q, qblk], axis=0).astype(BF16)
            m = None
            for j in range(i + 1):
                sc = _dot(k_scr[j], qpad) + bias_ref[hh, i - j]
                if j < i and sel[j] is not None:
                    sc = sc + sel[j]
                sc_scr[j] = sc
                mj = jnp.max(sc, axis=0, keepdims=True)
                m = mj if m is None else jnp.maximum(m, mj)
            for j in range(i + 1):
                p_scr[j * BLK:(j + 1) * BLK, :] = jnp.exp(sc_scr[j] - m).astype(BF16)
            acc = _dot(va_scr[hh, :, 0:(i + 1) * BLK], p_scr[0:(i + 1) * BLK, :])
            out = acc[0:HD] / acc[HD:HD + 1]
            o_ref[0, hh * HD:(hh + 1) * HD, i * BLK:(i + 1) * BLK] = out.astype(BF16)


def _attention(projt, bias, qg, kg):
    bsz, _, s = projt.shape
    nb = s // BLK
    rows = 2 * HD
    qb, kb, vb = Q0 // rows, K0 // rows, V0 // rows
    return pl.pallas_call(
        functools.partial(_attn_body, s=s),
        grid=(H // 2, bsz),
        in_specs=[pl.BlockSpec((1, rows, s), lambda p, b: (b, qb + p, 0)),
                  pl.BlockSpec((1, rows, s), lambda p, b: (b, kb + p, 0)),
                  pl.BlockSpec((1, rows, s), lambda p, b: (b, vb + p, 0)),
                  pl.BlockSpec((2, nb, BLK, BLK), lambda p, b: (p, 0, 0, 0)),
                  pl.BlockSpec((HD, 1), lambda p, b: (0, 0)),
                  pl.BlockSpec((HD, 1), lambda p, b: (0, 0))],
        out_specs=pl.BlockSpec((1, rows, s), lambda p, b: (b, p, 0)),
        out_shape=jax.ShapeDtypeStruct((bsz, ATTN_W, s), BF16),
        scratch_shapes=[pltpu.VMEM((nb, BLK, rows), BF16),
                        pltpu.VMEM((nb, BLK, BLK), F32),
                        pltpu.VMEM((s, BLK), BF16),
                        pltpu.VMEM((2, VA_ROWS, s), BF16)],
        compiler_params=_cparams("arbitrary", "arbitrary"),
        name="moba_attention",
    )(projt, projt, projt, bias, qg, kg)


def _split3(x):
    hi = x.astype(BF16)
    r1 = x - hi.astype(F32)
    mid = r1.astype(BF16)
    lo = (r1 - mid.astype(F32)).astype(BF16)
    return hi, mid, lo


def _ssd_body(z_ref, xs_ref, bc_ref, dtt_ref, dt_ref, cw_ref, cb_ref, dtbc_ref, dtbr_ref, alc_ref, alr_ref,
              dskip_ref, ng_ref, o_ref, prevx_scr, state_scr):
    @pl.when(pl.program_id(1) == 0)
    def _():
        prevx_scr[...] = jnp.zeros_like(prevx_scr)
        state_scr[...] = jnp.zeros_like(state_scr)

    cur = jnp.concatenate([xs_ref[0], bc_ref[0]], axis=0).astype(F32)
    prev = prevx_scr[...]
    lane = lax.broadcasted_iota(I32, (1, L), 1)
    acc = cb_ref[...] + cw_ref[SSM_K - 1] * cur
    for sft in range(1, SSM_K):
        shifted = pltpu.roll(jnp.where(lane >= L - sft, prev, cur), sft, 1)
        acc = acc + cw_ref[SSM_K - 1 - sft] * shifted
    prevx_scr[...] = cur
    xbc = acc * _sigmoid(acc)

    dtt = _softplus(dtt_ref[0] + dtbc_ref[...])
    dts = _softplus(dt_ref[0] + dtbr_ref[...])
    adtt = dtt * (-jnp.exp(alc_ref[...]))
    adts = dts * (-jnp.exp(alr_ref[...]))
    r_i = lax.broadcasted_iota(I32, (L, L), 0)
    c_i = lax.broadcasted_iota(I32, (L, L), 1)
    upper = jnp.where(r_i <= c_i, 1.0, 0.0).astype(BF16)
    lower = jnp.where(r_i >= c_i, 1.0, 0.0).astype(BF16)
    acst = sum(_dot(part, upper) for part in _split3(adtt))
    acs = sum(_dot(lower, part) for part in _split3(adts))
    causal = c_i >= r_i
    last = acst[:, L - 1:L]
    dte = jnp.exp(last - acst)
    cdec = jnp.broadcast_to(jnp.exp(last), (SSM_H, L))
    ein = jnp.exp(acst)

    for g in range(SSM_G):
        bt = xbc[SSM_IN + g * SSM_N:SSM_IN + (g + 1) * SSM_N]
        ct = xbc[SSM_IN + SSM_G * SSM_N + g * SSM_N:SSM_IN + SSM_G * SSM_N + (g + 1) * SSM_N]
        ctb = ct.astype(BF16)
        bmb = bt.T.astype(BF16)
        cbt = _dot(bmb, ctb)
        rows = slice(g * SSM_J * SSM_HD, (g + 1) * SSM_J * SSM_HD)
        xg = xbc[rows]
        state = state_scr[rows]
        yoff = _dot(state.astype(BF16), ctb)
        xdt, sin, cd, ys = [], [], [], []
        for j in range(SSM_J):
            h = g * SSM_J + j
            xj = xg[j * SSM_HD:(j + 1) * SSM_HD]
            xdj = xj * dtt[h:h + 1]
            xdt.append(xdj)
            sin.append(xdj * dte[h:h + 1])
            cd.append(jnp.broadcast_to(cdec[h:h + 1], (SSM_HD, SSM_N)))
        new_state = _dot(jnp.concatenate(sin, axis=0).astype(BF16), bmb)
        state_scr[rows] = state * jnp.concatenate(cd, axis=0) + new_state
        for j in range(SSM_J):
            h = g * SSM_J + j
            seg = acst[h:h + 1] - acs[:, h:h + 1]
            mt = cbt * jnp.exp(jnp.where(causal, seg, NEG_INF))
            ydiag = _dot(xdt[j].astype(BF16), mt.astype(BF16))
            ys.append(ydiag + yoff[j * SSM_HD:(j + 1) * SSM_HD] * ein[h:h + 1])
        y = jnp.concatenate(ys, axis=0) + dskip_ref[rows] * xg
        zg = z_ref[0, rows].astype(F32)
        y = y * (zg * _sigmoid(zg))
        ms = jnp.mean(y * y, axis=0, keepdims=True)
        o_ref[0, rows] = (y * lax.rsqrt(ms + EPS) * ng_ref[rows]).astype(BF16)


def _ssd(projt, dtt, dts, cw, cb, dtb, alog, dskip, ng):
    bsz, _, s = projt.shape
    nc = s // L
    full = lambda shape: pl.BlockSpec(shape, lambda b, c: (0,) * len(shape))
    return pl.pallas_call(
        _ssd_body,
        grid=(bsz, nc),
        in_specs=[pl.BlockSpec((1, SSM_IN, L), lambda b, c: (b, Z0 // SSM_IN, c)),
                  pl.BlockSpec((1, SSM_IN, L), lambda b, c: (b, X0 // SSM_IN, c)),
                  pl.BlockSpec((1, 2 * SSM_G * SSM_N, L), lambda b, c: (b, BC0 // (2 * SSM_G * SSM_N), c)),
                  pl.BlockSpec((1, SSM_H, L), lambda b, c: (b, 0, c)),
                  pl.BlockSpec((1, L, SSM_H), lambda b, c: (b, c, 0)),
                  full((SSM_K, CONV_DIM, L)), full((CONV_DIM, L)),
                  full((SSM_H, 1)), full((1, SSM_H)), full((SSM_H, 1)), full((1, SSM_H)),
                  full((SSM_IN, L)), full((SSM_IN, L))],
        out_specs=pl.BlockSpec((1, SSM_IN, L), lambda b, c: (b, 0, c)),
        out_shape=jax.ShapeDtypeStruct((bsz, SSM_IN, s), BF16),
        scratch_shapes=[pltpu.VMEM((CONV_DIM, L), F32), pltpu.VMEM((SSM_IN, SSM_N), F32)],
        compiler_params=_cparams("arbitrary", "arbitrary"),
        name="ssd_branch",
    )(projt, projt, projt, dtt, dts, cw, cb, dtb.reshape(SSM_H, 1), dtb.reshape(1, SSM_H),
      alog.reshape(SSM_H, 1), alog.reshape(1, SSM_H), dskip, ng)


def _merge_body(attn_ref, ssm_ref, gate_ref, x_ref, mod_ref, gb_ref, wat_ref, wst_ref, wo_ref, g2_ref,
                rwt_ref, rb_ref, x1_ref, u2_ref, lg_ref):
    a = _dot(wat_ref[...], attn_ref[0])
    sm = _dot(wst_ref[...], ssm_ref[0])
    gate = _sigmoid(gate_ref[0].astype(F32) + gb_ref[...])
    merged_t = gate[:D] * a + gate[D:] * sm
    merged = merged_t.T.astype(BF16)
    y = _dot(merged, wo_ref[...])
    x1 = x_ref[0] + mod_ref[0, 2:3, :] * y
    x1_ref[0] = x1
    ms = jnp.mean(x1 * x1, axis=-1, keepdims=True)
    u2 = (x1 * lax.rsqrt(ms + EPS) * g2_ref[...]) * (1.0 + mod_ref[0, 4:5, :]) + mod_ref[0, 3:4, :]
    u2_ref[0] = u2
    lg_ref[...] = _dot_nt(rwt_ref[...], u2.astype(BF16)) + rb_ref[...]


def _merge(attnt, ssmt, projt, x, mod, gb, wat, wst, wo, g2, rwt, rb):
    bsz, s, _ = x.shape
    tm = 512
    nm = s // tm
    full = lambda shape: pl.BlockSpec(shape, lambda b, m: (0,) * len(shape))
    return pl.pallas_call(
        _merge_body,
        grid=(bsz, nm),
        in_specs=[pl.BlockSpec((1, ATTN_W, tm), lambda b, m: (b, 0, m)),
                  pl.BlockSpec((1, SSM_IN, tm), lambda b, m: (b, 0, m)),
                  pl.BlockSpec((1, 2 * D, tm), lambda b, m: (b, G0 // (2 * D), m)),
                  pl.BlockSpec((1, tm, D), lambda b, m: (b, m, 0)),
                  pl.BlockSpec((1, 6, D), lambda b, m: (b, 0, 0)),
                  full((2 * D, 1)), full((D, ATTN_W)), full((D, SSM_IN)), full((D, D)), full((1, D)),
                  full((NE, D)), full((NE, 1))],
        out_specs=[pl.BlockSpec((1, tm, D), lambda b, m: (b, m, 0)),
                   pl.BlockSpec((1, tm, D), lambda b, m: (b, m, 0)),
                   pl.BlockSpec((NE, tm), lambda b, m: (0, b * nm + m))],
        out_shape=[jax.ShapeDtypeStruct((bsz, s, D), F32),
                   jax.ShapeDtypeStruct((bsz, s, D), F32),
                   jax.ShapeDtypeStruct((NE, bsz * s), F32)],
        compiler_params=_cparams("arbitrary", "arbitrary"),
        name="merge_outproj",
    )(attnt, ssmt, projt, x, mod, gb, wat, wst, wo, g2, rwt, rb)


def _route_body(lg_ref, e_ref, w_ref, r_ref, cnt_ref, carry_scr, *, tm):
    @pl.when(pl.program_id(0) == 0)
    def _():
        carry_scr[...] = jnp.zeros_like(carry_scr)

    work = lg_ref[...]
    eidx = lax.broadcasted_iota(I32, (NE, tm), 0).astype(F32)
    vals, idxs, hots = [], [], []
    for _ in range(TOPE):
        m = jnp.max(work, axis=0, keepdims=True)
        idx = jnp.min(jnp.where(work == m, eidx, float(NE)), axis=0, keepdims=True)
        hot = eidx == idx
        work = jnp.where(hot, NEG_INF, work)
        vals.append(m)
        idxs.append(idx)
        hots.append(hot)
    ex = [jnp.exp(v - vals[0]) for v in vals]
    den = ex[0] + ex[1] + ex[2] + ex[3]
    multi = jnp.zeros((NE, tm), F32)
    for hot in hots:
        multi = multi + jnp.where(hot, 1.0, 0.0)
    r_i = lax.broadcasted_iota(I32, (tm, tm), 0)
    c_i = lax.broadcasted_iota(I32, (tm, tm), 1)
    strict = jnp.where(r_i < c_i, 1.0, 0.0).astype(BF16)
    before = _dot(multi.astype(BF16), strict) + carry_scr[:, 0:1]
    for k in range(TOPE):
        e_ref[k:k + 1, :] = idxs[k].astype(I32)
        w_ref[k:k + 1, :] = ex[k] / den
        r_ref[k:k + 1, :] = jnp.sum(jnp.where(hots[k], before, 0.0), axis=0, keepdims=True).astype(I32)
    total = carry_scr[...] + jnp.sum(multi, axis=1, keepdims=True)
    carry_scr[...] = total
    cnt_ref[...] = total


def _route(logits_t):
    t = logits_t.shape[1]
    tm = 512
    return pl.pallas_call(
        functools.partial(_route_body, tm=tm),
        grid=(t // tm,),
        in_specs=[pl.BlockSpec((NE, tm), lambda i: (0, i))],
        out_specs=[pl.BlockSpec((TOPE, tm), lambda i: (0, i)),
                   pl.BlockSpec((TOPE, tm), lambda i: (0, i)),
                   pl.BlockSpec((TOPE, tm), lambda i: (0, i)),
                   pl.BlockSpec((NE, 128), lambda i: (0, 0))],
        out_shape=[jax.ShapeDtypeStruct((TOPE, t), I32),
                   jax.ShapeDtypeStruct((TOPE, t), F32),
                   jax.ShapeDtypeStruct((TOPE, t), I32),
                   jax.ShapeDtypeStruct((NE, 128), F32)],
        scratch_shapes=[pltpu.VMEM((NE, 128), F32)],
        compiler_params=_cparams("arbitrary"),
        name="route_top4",
    )(logits_t)


def _row_copy(src_hbm, src_row, dst, dst_row, sem):
    return pltpu.make_async_copy(src_hbm.at[pl.ds(src_row, 1)], dst.at[pl.ds(dst_row, 1)], sem)


def _dispatch_body(pe_ref, nu_ref, dest_ref, u_ref, o_hbm, zero_scr, sem, *, nblk):
    def fill(blk):
        return pltpu.make_async_copy(zero_scr, o_hbm.at[pl.ds(blk * MOE_ROWS, MOE_ROWS)], sem)

    @pl.when(pl.program_id(0) == 0)
    def _():
        zero_scr[...] = jnp.zeros_like(zero_scr)
        pad_blocks = []
        for e in range(NE):
            prev_end = pe_ref[e - 1] if e > 0 else 0
            pad_blocks.append((pe_ref[e] > prev_end, pe_ref[e] // MOE_ROWS - 1))
        for b in range(NE):
            pad_blocks.append((nblk - 1 - b >= nu_ref[0], nblk - 1 - b))
        for cond, blk in pad_blocks:
            @pl.when(cond)
            def _(blk=blk):
                fill(blk).start()
        for cond, blk in pad_blocks:
            @pl.when(cond)
            def _(blk=blk):
                fill(blk).wait()

    def start(r, c):
        for k in range(TOPE):
            _row_copy(u_ref, r, o_hbm, dest_ref[0, k, r], sem).start(priority=k % 2)
        return c

    def wait(r, c):
        for k in range(TOPE):
            _row_copy(u_ref, 0, o_hbm, 0, sem).wait()
        return c

    lax.fori_loop(0, COMBINE_TOK, start, 0)
    lax.fori_loop(0, COMBINE_TOK, wait, 0)


def _dispatch(u2, dest_blocks, p_end, n_used, cap):
    t = u2.shape[0]
    tm = COMBINE_TOK
    nblk = cap // MOE_ROWS
    grid_spec = pltpu.PrefetchScalarGridSpec(
        num_scalar_prefetch=2,
        grid=(t // tm,),
        in_specs=[pl.BlockSpec((1, TOPE, tm), lambda i, pe, nu: (i, 0, 0), memory_space=pltpu.SMEM),
                  pl.BlockSpec((tm, D), lambda i, pe, nu: (i, 0))],
        out_specs=pl.BlockSpec(memory_space=pl.ANY),
        scratch_shapes=[pltpu.VMEM((MOE_ROWS, D), F32), pltpu.SemaphoreType.DMA(())],
    )
    return pl.pallas_call(
        functools.partial(_dispatch_body, nblk=nblk),
        grid_spec=grid_spec,
        out_shape=jax.ShapeDtypeStruct((cap, D), F32),
        compiler_params=_cparams("arbitrary"),
        name="moe_dispatch",
    )(p_end, n_used, dest_blocks, u2)


def _expert_body(be_ref, nu_ref, x_ref, w1_ref, b1_ref, w2_ref, b2_ref, o_ref, w1_scr, w2_scr):
    i = pl.program_id(0)
    changed = jnp.logical_or(i == 0, be_ref[i] != be_ref[jnp.maximum(i - 1, 0)])

    @pl.when(changed)
    def _():
        w1_scr[...] = w1_ref[0].astype(BF16)
        w2_scr[...] = w2_ref[0].astype(BF16)

    @pl.when(i < nu_ref[0])
    def _():
        h = _dot(x_ref[...].astype(BF16), w1_scr[...]) + b1_ref[0]
        glu = jnp.minimum(h[:, :FF], LIMIT)
        lin = jnp.clip(h[:, FF:], -LIMIT, LIMIT)
        act = glu * _sigmoid(ALPHA * glu) * (lin + 1.0)
        o_ref[...] = _dot(act.astype(BF16), w2_scr[...]) + b2_ref[0]

    @pl.when(i >= nu_ref[0])
    def _():
        o_ref[...] = jnp.zeros_like(o_ref)


def _experts(xs, block_e, n_used, w1, b1, w2, b2):
    cap = xs.shape[0]
    nblk = cap // MOE_ROWS
    grid_spec = pltpu.PrefetchScalarGridSpec(
        num_scalar_prefetch=2,
        grid=(nblk,),
        in_specs=[pl.BlockSpec((MOE_ROWS, D), lambda i, be, nu: (i, 0)),
                  pl.BlockSpec((1, D, 2 * FF), lambda i, be, nu: (be[i], 0, 0)),
                  pl.BlockSpec((1, 1, 2 * FF), lambda i, be, nu: (be[i], 0, 0)),
                  pl.BlockSpec((1, FF, D), lambda i, be, nu: (be[i], 0, 0)),
                  pl.BlockSpec((1, 1, D), lambda i, be, nu: (be[i], 0, 0))],
        out_specs=pl.BlockSpec((MOE_ROWS, D), lambda i, be, nu: (i, 0)),
        scratch_shapes=[pltpu.VMEM((D, 2 * FF), BF16), pltpu.VMEM((FF, D), BF16)],
    )
    return pl.pallas_call(
        _expert_body,
        grid_spec=grid_spec,
        out_shape=jax.ShapeDtypeStruct((cap, D), F32),
        compiler_params=_cparams("arbitrary"),
        name="moe_experts",
    )(block_e, n_used, xs, w1, b1.reshape(NE, 1, 2 * FF), w2, b2.reshape(NE, 1, D))


def _combine_body(dest_ref, ys_hbm, w_ref, x1_ref, g2_ref, o_ref, buf, sem):
    def start(r, c):
        for k in range(TOPE):
            _row_copy(ys_hbm, dest_ref[0, k, r], buf.at[k], r, sem).start(priority=k % 2)
        return c

    def wait(r, c):
        for k in range(TOPE):
            _row_copy(ys_hbm, 0, buf.at[k], 0, sem).wait()
        return c

    lax.fori_loop(0, COMBINE_TOK, start, 0)
    lax.fori_loop(0, COMBINE_TOK, wait, 0)
    w = w_ref[...]
    ffn = buf[0] * w[:, 0:1]
    for k in range(1, TOPE):
        ffn = ffn + buf[k] * w[:, k:k + 1]
    o_ref[0] = x1_ref[0] + g2_ref[0] * ffn


def _combine(ys, dest_blocks, top_w, x1, gate2):
    bsz, s, _ = x1.shape
    tm = COMBINE_TOK
    nm = s // tm
    return pl.pallas_call(
        _combine_body,
        grid=(bsz, nm),
        in_specs=[pl.BlockSpec((1, TOPE, tm), lambda b, m: (b * nm + m, 0, 0), memory_space=pltpu.SMEM),
                  pl.BlockSpec(memory_space=pl.ANY),
                  pl.BlockSpec((tm, TOPE), lambda b, m: (b * nm + m, 0)),
                  pl.BlockSpec((1, tm, D), lambda b, m: (b, m, 0)),
                  pl.BlockSpec((1, 1, D), lambda b, m: (b, 0, 0))],
        out_specs=pl.BlockSpec((1, tm, D), lambda b, m: (b, m, 0)),
        out_shape=jax.ShapeDtypeStruct((bsz, s, D), F32),
        scratch_shapes=[pltpu.VMEM((TOPE, tm, D), F32), pltpu.SemaphoreType.DMA(())],
        compiler_params=_cparams("arbitrary", "arbitrary"),
        name="moe_combine",
    )(dest_blocks, ys, top_w.T, x1, gate2)


def _layer(x, mod, norm1_g, w_in, q_norm_g, k_norm_g, rel_bias_table, conv_w, conv_b, dt_bias, a_log, d_skip,
           ssm_norm_g, w_attn_branch, w_ssm_branch, gate_bias, w_out, norm2_g, router_w, router_b,
           expert_w1, expert_b1, expert_w2, expert_b2):
    bsz, s, _ = x.shape
    t = bsz * s
    assert s % BLK == 0 and s % 512 == 0
    mod3 = mod.reshape(bsz, 6, D)

    c0 = 3 * ATTN_W
    c_x, c_b, c_dt, c_g = c0 + SSM_IN, c0 + SSM_IN + SSM_IN, c0 + SSM_IN + CONV_DIM, c0 + SSM_IN + CONV_DIM + SSM_H
    w_rows = jnp.concatenate([w_in[:, c0:c_x], w_in[:, c_g:], w_in[:, c_x:c_b], w_in[:, :c0], w_in[:, c_b:c_dt]],
                             axis=1)
    wt = w_rows.T.astype(BF16)
    wdt = w_in[:, c_dt:c_g].astype(BF16)
    projt, dtt, dts = _inproj(x, mod3, norm1_g.reshape(1, D), wt, wdt.T, wdt)

    bias = _bias_tiles(rel_bias_table.T.astype(F32), s // BLK)
    attnt = _attention(projt, bias, q_norm_g.reshape(HD, 1), k_norm_g.reshape(HD, 1))

    lanes = lambda v: jnp.broadcast_to(v[..., None], v.shape + (L,))
    ssmt = _ssd(projt, dtt, dts, lanes(conv_w), lanes(conv_b), dt_bias, a_log,
                lanes(jnp.repeat(d_skip, SSM_HD)), lanes(ssm_norm_g))

    x1, u2, logits_t = _merge(attnt, ssmt, projt, x, mod3, gate_bias.reshape(2 * D, 1),
                              w_attn_branch.T.astype(BF16), w_ssm_branch.T.astype(BF16), w_out.astype(BF16),
                              norm2_g.reshape(1, D), router_w.T.astype(BF16), router_b.reshape(NE, 1))

    top_e, top_w, rank, cnt = _route(logits_t)
    counts = cnt[:, 0].astype(I32)
    padded = (counts + MOE_ROWS - 1) // MOE_ROWS * MOE_ROWS
    p_end = jnp.cumsum(padded)
    p_start = p_end - padded
    experts = jnp.arange(NE, dtype=I32)
    dest = rank + jnp.sum(jnp.where(top_e[..., None] == experts, p_start, 0), axis=-1)
    dest_blocks = dest.reshape(TOPE, t // COMBINE_TOK, COMBINE_TOK).transpose(1, 0, 2)
    cap = TOPE * t + NE * MOE_ROWS
    nblk = cap // MOE_ROWS
    blk_start = jnp.arange(nblk, dtype=I32) * MOE_ROWS
    block_e = jnp.minimum(jnp.sum((p_end[None, :] <= blk_start[:, None]).astype(I32), axis=1), NE - 1)
    n_used = (p_end[-1] // MOE_ROWS).astype(I32).reshape(1)

    xs = _dispatch(u2.reshape(t, D), dest_blocks, p_end.astype(I32), n_used, cap)
    ys = _experts(xs, block_e, n_used, expert_w1, expert_b1, expert_w2, expert_b2)
    return _combine(ys, dest_blocks, top_w, x1, mod3[:, 5:6, :])


def kernel(x, c, ada_w, ada_b, norm1_g, w_in, q_norm_g, k_norm_g, rel_bias_table, conv_w, conv_b, dt_bias, a_log,
           d_skip, ssm_norm_g, w_attn_branch, w_ssm_branch, gate_bias, w_out, norm2_g, router_w, router_b,
           expert_w1, expert_b1, expert_w2, expert_b2):
    h = x
    for l in range(ada_w.shape[0]):
        mod = _ada(c, ada_w[l], ada_b[l])
        h = _layer(h, mod, norm1_g[l], w_in[l], q_norm_g[l], k_norm_g[l], rel_bias_table, conv_w[l], conv_b[l],
                   dt_bias[l], a_log[l], d_skip[l], ssm_norm_g[l], w_attn_branch[l], w_ssm_branch[l],
                   gate_bias[l], w_out[l], norm2_g[l], router_w[l], router_b[l], expert_w1[l], expert_b1[l],
                   expert_w2[l], expert_b2[l])
    return h
```

```python
import functools
import math
from typing import NamedTuple

import numpy as np
import jax
import jax.numpy as jnp
from jax import lax
from jax.experimental import pallas as pl
from jax.experimental.pallas import tpu as pltpu

F32, BF16, I32 = jnp.float32, jnp.bfloat16, jnp.int32

D = 1024
H = 16
HD = 64
ATTN_W = H * HD
SCALE = HD ** -0.5
BLK = 256
TOPK = 3
NBUCKET = 32
REL_EXACT = 16
REL_MAXD = 1024
SSM_IN = 2048
SSM_HD = 64
SSM_H = SSM_IN // SSM_HD
SSM_G = 4
SSM_J = SSM_H // SSM_G
SSM_N = 128
SSM_K = 4
L = 128
CONV_DIM = SSM_IN + 2 * SSM_G * SSM_N
NE = 32
TOPE = 4
FF = D
LIMIT = 7.0
ALPHA = 1.702
EPS = 1e-6
NEG_INF = float("-inf")

Z0, G0, X0, Q0, K0, V0, BC0 = 0, 2048, 4096, 6144, 7168, 8192, 9216
NPROJ = 10240

VA_ROWS = HD + 16
MOE_ROWS = 512
COMBINE_TOK = 256
FUSE_TOK = MOE_ROWS // TOPE
ZERO_ROWS = 128
VMEM_LIMIT = 56 * 1024 * 1024


def _dot(a, b):
    return jnp.dot(a, b, preferred_element_type=F32)


def _dot_nt(a, b):
    return lax.dot_general(a, b, (((1,), (1,)), ((), ())), preferred_element_type=F32)


def _cparams(*sem):
    return pltpu.CompilerParams(dimension_semantics=sem, vmem_limit_bytes=VMEM_LIMIT)


def _sigmoid(x):
    return 0.5 * jnp.tanh(0.5 * x) + 0.5


def _softplus(x):
    return jnp.maximum(x, 0.0) + jnp.log1p(jnp.exp(-jnp.abs(x)))


def _ada_body(c_ref, w_ref, b_ref, o_ref):
    c = c_ref[...]
    ca = c * _sigmoid(c)
    o_ref[...] = _dot(ca.astype(BF16), w_ref[...].astype(BF16)) + b_ref[...]


def _ada(c, w, b):
    bsz = c.shape[0]
    n = w.shape[1]
    tn = 1536
    return pl.pallas_call(
        _ada_body,
        grid=(n // tn,),
        in_specs=[pl.BlockSpec((bsz, D), lambda j: (0, 0)),
                  pl.BlockSpec((D, tn), lambda j: (0, j)),
                  pl.BlockSpec((1, tn), lambda j: (0, j))],
        out_specs=pl.BlockSpec((bsz, tn), lambda j: (0, j)),
        out_shape=jax.ShapeDtypeStruct((bsz, n), F32),
        compiler_params=_cparams("arbitrary"),
        name="ada_mod",
    )(c, w, b.reshape(1, n))


def _inproj_body(x_ref, mod_ref, g_ref, w_ref, wdtt_ref, wdt_ref, o_ref, dtt_ref, dt_ref, u_scr):
    @pl.when(pl.program_id(2) == 0)
    def _():
        rows = 256
        for r in range(x_ref.shape[1] // rows):
            sl = slice(r * rows, (r + 1) * rows)
            x = x_ref[0, sl, :]
            ms = jnp.mean(x * x, axis=-1, keepdims=True)
            xn = x * lax.rsqrt(ms + EPS) * g_ref[...]
            u = xn * (1.0 + mod_ref[0, 1:2, :]) + mod_ref[0, 0:1, :]
            ub = u.astype(BF16)
            u_scr[sl, :] = ub
            dtt_ref[0, :, sl] = _dot_nt(wdtt_ref[...], ub)
            dt_ref[0, sl, :] = _dot(ub, wdt_ref[...])

    o_ref[0] = _dot_nt(w_ref[...], u_scr[...]).astype(BF16)


def _inproj(x, mod, g1, wt, wdtt, wdt):
    bsz, s, _ = x.shape
    tm = s
    tn = 1024
    return pl.pallas_call(
        _inproj_body,
        grid=(bsz, s // tm, NPROJ // tn),
        in_specs=[pl.BlockSpec((1, tm, D), lambda b, m, n: (b, m, 0)),
                  pl.BlockSpec((1, 6, D), lambda b, m, n: (b, 0, 0)),
                  pl.BlockSpec((1, D), lambda b, m, n: (0, 0)),
                  pl.BlockSpec((tn, D), lambda b, m, n: (n, 0)),
                  pl.BlockSpec((SSM_H, D), lambda b, m, n: (0, 0)),
                  pl.BlockSpec((D, SSM_H), lambda b, m, n: (0, 0))],
        out_specs=[pl.BlockSpec((1, tn, tm), lambda b, m, n: (b, n, m)),
                   pl.BlockSpec((1, SSM_H, tm), lambda b, m, n: (b, 0, m)),
                   pl.BlockSpec((1, tm, SSM_H), lambda b, m, n: (b, m, 0))],
        out_shape=[jax.ShapeDtypeStruct((bsz, NPROJ, s), BF16),
                   jax.ShapeDtypeStruct((bsz, SSM_H, s), F32),
                   jax.ShapeDtypeStruct((bsz, s, SSM_H), F32)],
        scratch_shapes=[pltpu.VMEM((tm, D), BF16)],
        compiler_params=_cparams("arbitrary", "arbitrary", "arbitrary"),
        name="in_proj",
    )(x, mod, g1, wt, wdtt, wdt)


def _t5_thresholds(max_dist):
    d = np.arange(max_dist, dtype=np.int64)
    df = np.maximum(d, 1).astype(np.float32)
    large = REL_EXACT + (np.log(df / np.float32(REL_EXACT)) / np.float32(math.log(REL_MAXD / REL_EXACT))
                         * np.float32(NBUCKET - REL_EXACT)).astype(np.int32)
    bucket = np.where(d < REL_EXACT, d, np.minimum(large, NBUCKET - 1))
    return [int(np.argmax(bucket >= b)) if np.any(bucket >= b) else int(max_dist) for b in range(NBUCKET)]


def _bias_body(tab_ref, o_ref, *, thr):
    h = pl.program_id(0)
    delta = pl.program_id(1)
    ki = lax.broadcasted_iota(I32, (BLK, BLK), 0)
    qi = lax.broadcasted_iota(I32, (BLK, BLK), 1)
    d = jnp.maximum(delta * BLK + qi - ki, 0)
    val = jnp.full((BLK, BLK), tab_ref[h, 0], F32)
    for b in range(1, NBUCKET):
        val = jnp.where(d >= thr[b], tab_ref[h, b], val)
    o_ref[0, 0] = jnp.where(jnp.logical_or(delta > 0, qi >= ki), val, NEG_INF)


def _bias_tiles(table_t, nb):
    thr = _t5_thresholds(nb * BLK + BLK)
    return pl.pallas_call(
        functools.partial(_bias_body, thr=thr),
        grid=(H, nb),
        in_specs=[pl.BlockSpec(memory_space=pltpu.SMEM)],
        out_specs=pl.BlockSpec((1, 1, BLK, BLK), lambda h, d: (h, d, 0, 0)),
        out_shape=jax.ShapeDtypeStruct((H, nb, BLK, BLK), F32),
        compiler_params=_cparams("arbitrary", "arbitrary"),
        name="t5_bias_tiles",
    )(table_t)


def _attn_body(q_ref, k_ref, v_ref, bias_ref, qg_ref, kg_ref, o_ref, k_scr, sc_scr, p_scr, va_scr, *, s):
    nb = s // BLK

    def normed(ref, g_ref, hh):
        t = ref[0, hh * HD:(hh + 1) * HD, :].astype(F32)
        ms = jnp.mean(t * t, axis=0, keepdims=True)
        return t * lax.rsqrt(ms + EPS) * g_ref[...]

    qn = [normed(q_ref, qg_ref, hh) for hh in range(2)]
    kn = [normed(k_ref, kg_ref, hh) for hh in range(2)]
    kpair = jnp.concatenate(kn, axis=0)
    for j in range(nb):
        k_scr[j] = kpair[:, j * BLK:(j + 1) * BLK].T.astype(BF16)
    zeros_q = jnp.zeros((HD, BLK), F32)
    row = lax.broadcasted_iota(I32, (VA_ROWS - HD, s), 0)
    ones_row = jnp.where(row == 0, 1.0, 0.0).astype(BF16)
    for hh in range(2):
        va_scr[hh, 0:HD, :] = v_ref[0, hh * HD:(hh + 1) * HD, :]
        va_scr[hh, HD:VA_ROWS, :] = ones_row

    for hh in range(2):
        gate = []
        for j in range(nb):
            kmean = jnp.mean(kn[hh][:, j * BLK:(j + 1) * BLK], axis=1, keepdims=True)
            gate.append(jnp.sum(qn[hh] * kmean, axis=0, keepdims=True))
        qs = qn[hh] * SCALE
        for i in range(nb):
            sel = [None] * i
            if i > TOPK:
                rows = [g[:, i * BLK:(i + 1) * BLK] for g in gate[:i]]
                for j in range(i):
                    rank = jnp.zeros((1, BLK), F32)
                    for jj in range(i):
                        if jj != j:
                            beats = (rows[jj] >= rows[j]) if jj < j else (rows[jj] > rows[j])
                            rank = rank + jnp.where(beats, 1.0, 0.0)
                    sel[j] = jnp.where(rank < TOPK, 0.0, NEG_INF)
            qblk = qs[:, i * BLK:(i + 1) * BLK]
            qpad = jnp.concatenate([qblk, zeros_q] if hh == 0 else [zeros_q, qblk], axis=0).astype(BF16)
            m = None
            for j in range(i + 1):
                sc = _dot(k_scr[j], qpad) + bias_ref[hh, i - j]
                if j < i and sel[j] is not None:
                    sc = sc + sel[j]
                sc_scr[j] = sc
                mj = jnp.max(sc, axis=0, keepdims=True)
                m = mj if m is None else jnp.maximum(m, mj)
            for j in range(i + 1):
                p_scr[j * BLK:(j + 1) * BLK, :] = jnp.exp(sc_scr[j] - m).astype(BF16)
            acc = _dot(va_scr[hh, :, 0:(i + 1) * BLK], p_scr[0:(i + 1) * BLK, :])
            out = acc[0:HD] / acc[HD:HD + 1]
            o_ref[0, hh * HD:(hh + 1) * HD, i * BLK:(i + 1) * BLK] = out.astype(BF16)


def _attention(projt, bias, qg, kg):
    bsz, _, s = projt.shape
    nb = s // BLK
    rows = 2 * HD
    qb, kb, vb = Q0 // rows, K0 // rows, V0 // rows
    return pl.pallas_call(
        functools.partial(_attn_body, s=s),
        grid=(H // 2, bsz),
        in_specs=[pl.BlockSpec((1, rows, s), lambda p, b: (b, qb + p, 0)),
                  pl.BlockSpec((1, rows, s), lambda p, b: (b, kb + p, 0)),
                  pl.BlockSpec((1, rows, s), lambda p, b: (b, vb + p, 0)),
                  pl.BlockSpec((2, nb, BLK, BLK), lambda p, b: (p, 0, 0, 0)),
                  pl.BlockSpec((HD, 1), lambda p, b: (0, 0)),
                  pl.BlockSpec((HD, 1), lambda p, b: (0, 0))],
        out_specs=pl.BlockSpec((1, rows, s), lambda p, b: (b, p, 0)),
        out_shape=jax.ShapeDtypeStruct((bsz, ATTN_W, s), BF16),
        scratch_shapes=[pltpu.VMEM((nb, BLK, rows), BF16),
                        pltpu.VMEM((nb, BLK, BLK), F32),
                        pltpu.VMEM((s, BLK), BF16),
                        pltpu.VMEM((2, VA_ROWS, s), BF16)],
        compiler_params=_cparams("arbitrary", "arbitrary"),
        name="moba_attention",
    )(projt, projt, projt, bias, qg, kg)


def _split3(x):
    hi = x.astype(BF16)
    r1 = x - hi.astype(F32)
    mid = r1.astype(BF16)
    lo = (r1 - mid.astype(F32)).astype(BF16)
    return hi, mid, lo


def _ssd_body(z_ref, xs_ref, bc_ref, dtt_ref, dt_ref, cw_ref, cb_ref, dtbc_ref, dtbr_ref, alc_ref, alr_ref,
              dskip_ref, ng_ref, o_ref, prevx_scr, state_scr):
    @pl.when(pl.program_id(1) == 0)
    def _():
        prevx_scr[...] = jnp.zeros_like(prevx_scr)
        state_scr[...] = jnp.zeros_like(state_scr)

    cur = jnp.concatenate([xs_ref[0], bc_ref[0]], axis=0).astype(F32)
    prev = prevx_scr[...]
    lane = lax.broadcasted_iota(I32, (1, L), 1)
    acc = cb_ref[...] + cw_ref[SSM_K - 1] * cur
    for sft in range(1, SSM_K):
        shifted = pltpu.roll(jnp.where(lane >= L - sft, prev, cur), sft, 1)
        acc = acc + cw_ref[SSM_K - 1 - sft] * shifted
    prevx_scr[...] = cur
    xbc = acc * _sigmoid(acc)

    dtt = _softplus(dtt_ref[0] + dtbc_ref[...])
    dts = _softplus(dt_ref[0] + dtbr_ref[...])
    adtt = dtt * (-jnp.exp(alc_ref[...]))
    adts = dts * (-jnp.exp(alr_ref[...]))
    r_i = lax.broadcasted_iota(I32, (L, L), 0)
    c_i = lax.broadcasted_iota(I32, (L, L), 1)
    upper = jnp.where(r_i <= c_i, 1.0, 0.0).astype(BF16)
    lower = jnp.where(r_i >= c_i, 1.0, 0.0).astype(BF16)
    acst = sum(_dot(part, upper) for part in _split3(adtt))
    acs = sum(_dot(lower, part) for part in _split3(adts))
    causal = c_i >= r_i
    last = acst[:, L - 1:L]
    dte = jnp.exp(last - acst)
    cdec = jnp.broadcast_to(jnp.exp(last), (SSM_H, L))
    ein = jnp.exp(acst)

    for g in range(SSM_G):
        bt = xbc[SSM_IN + g * SSM_N:SSM_IN + (g + 1) * SSM_N]
        ct = xbc[SSM_IN + SSM_G * SSM_N + g * SSM_N:SSM_IN + SSM_G * SSM_N + (g + 1) * SSM_N]
        ctb = ct.astype(BF16)
        bmb = bt.T.astype(BF16)
        cbt = _dot(bmb, ctb)
        rows = slice(g * SSM_J * SSM_HD, (g + 1) * SSM_J * SSM_HD)
        xg = xbc[rows]
        state = state_scr[rows]
        yoff = _dot(state.astype(BF16), ctb)
        xdt, sin, cd, ys = [], [], [], []
        for j in range(SSM_J):
            h = g * SSM_J + j
            xj = xg[j * SSM_HD:(j + 1) * SSM_HD]
            xdj = xj * dtt[h:h + 1]
            xdt.append(xdj)
            sin.append(xdj * dte[h:h + 1])
            cd.append(jnp.broadcast_to(cdec[h:h + 1], (SSM_HD, SSM_N)))
        new_state = _dot(jnp.concatenate(sin, axis=0).astype(BF16), bmb)
        state_scr[rows] = state * jnp.concatenate(cd, axis=0) + new_state
        for j in range(SSM_J):
            h = g * SSM_J + j
            seg = acst[h:h + 1] - acs[:, h:h + 1]
            mt = cbt * jnp.exp(jnp.where(causal, seg, NEG_INF))
            ydiag = _dot(xdt[j].astype(BF16), mt.astype(BF16))
            ys.append(ydiag + yoff[j * SSM_HD:(j + 1) * SSM_HD] * ein[h:h + 1])
        y = jnp.concatenate(ys, axis=0) + dskip_ref[rows] * xg
        zg = z_ref[0, rows].astype(F32)
        y = y * (zg * _sigmoid(zg))
        ms = jnp.mean(y * y, axis=0, keepdims=True)
        o_ref[0, rows] = (y * lax.rsqrt(ms + EPS) * ng_ref[rows]).astype(BF16)


def _ssd(projt, dtt, dts, cw, cb, dtb, alog, dskip, ng):
    bsz, _, s = projt.shape
    nc = s // L
    full = lambda shape: pl.BlockSpec(shape, lambda b, c: (0,) * len(shape))
    return pl.pallas_call(
        _ssd_body,
        grid=(bsz, nc),
        in_specs=[pl.BlockSpec((1, SSM_IN, L), lambda b, c: (b, Z0 // SSM_IN, c)),
                  pl.BlockSpec((1, SSM_IN, L), lambda b, c: (b, X0 // SSM_IN, c)),
                  pl.BlockSpec((1, 2 * SSM_G * SSM_N, L), lambda b, c: (b, BC0 // (2 * SSM_G * SSM_N), c)),
                  pl.BlockSpec((1, SSM_H, L), lambda b, c: (b, 0, c)),
                  pl.BlockSpec((1, L, SSM_H), lambda b, c: (b, c, 0)),
                  full((SSM_K, CONV_DIM, L)), full((CONV_DIM, L)),
                  full((SSM_H, 1)), full((1, SSM_H)), full((SSM_H, 1)), full((1, SSM_H)),
                  full((SSM_IN, L)), full((SSM_IN, L))],
        out_specs=pl.BlockSpec((1, SSM_IN, L), lambda b, c: (b, 0, c)),
        out_shape=jax.ShapeDtypeStruct((bsz, SSM_IN, s), BF16),
        scratch_shapes=[pltpu.VMEM((CONV_DIM, L), F32), pltpu.VMEM((SSM_IN, SSM_N), F32)],
        compiler_params=_cparams("arbitrary", "arbitrary"),
        name="ssd_branch",
    )(projt, projt, projt, dtt, dts, cw, cb, dtb.reshape(SSM_H, 1), dtb.reshape(1, SSM_H),
      alog.reshape(SSM_H, 1), alog.reshape(1, SSM_H), dskip, ng)


def _merge_body(attn_ref, ssm_ref, gate_ref, x_ref, mod_ref, gb_ref, wat_ref, wst_ref, wo_ref, g2_ref,
                rwt_ref, rb_ref, x1_ref, u2_ref, lg_ref):
    a = _dot(wat_ref[...], attn_ref[0])
    sm = _dot(wst_ref[...], ssm_ref[0])
    gate = _sigmoid(gate_ref[0].astype(F32) + gb_ref[...])
    merged_t = gate[:D] * a + gate[D:] * sm
    merged = merged_t.T.astype(BF16)
    y = _dot(merged, wo_ref[...])
    x1 = x_ref[0] + mod_ref[0, 2:3, :] * y
    x1_ref[0] = x1
    ms = jnp.mean(x1 * x1, axis=-1, keepdims=True)
    u2 = (x1 * lax.rsqrt(ms + EPS) * g2_ref[...]) * (1.0 + mod_ref[0, 4:5, :]) + mod_ref[0, 3:4, :]
    u2_ref[0] = u2
    lg_ref[...] = _dot_nt(rwt_ref[...], u2.astype(BF16)) + rb_ref[...]


def _merge(attnt, ssmt, projt, x, mod, gb, wat, wst, wo, g2, rwt, rb):
    bsz, s, _ = x.shape
    tm = 512
    nm = s // tm
    full = lambda shape: pl.BlockSpec(shape, lambda b, m: (0,) * len(shape))
    return pl.pallas_call(
        _merge_body,
        grid=(bsz, nm),
        in_specs=[pl.BlockSpec((1, ATTN_W, tm), lambda b, m: (b, 0, m)),
                  pl.BlockSpec((1, SSM_IN, tm), lambda b, m: (b, 0, m)),
                  pl.BlockSpec((1, 2 * D, tm), lambda b, m: (b, G0 // (2 * D), m)),
                  pl.BlockSpec((1, tm, D), lambda b, m: (b, m, 0)),
                  pl.BlockSpec((1, 6, D), lambda b, m: (b, 0, 0)),
                  full((2 * D, 1)), full((D, ATTN_W)), full((D, SSM_IN)), full((D, D)), full((1, D)),
                  full((NE, D)), full((NE, 1))],
        out_specs=[pl.BlockSpec((1, tm, D), lambda b, m: (b, m, 0)),
                   pl.BlockSpec((1, tm, D), lambda b, m: (b, m, 0)),
                   pl.BlockSpec((NE, tm), lambda b, m: (0, b * nm + m))],
        out_shape=[jax.ShapeDtypeStruct((bsz, s, D), F32),
                   jax.ShapeDtypeStruct((bsz, s, D), F32),
                   jax.ShapeDtypeStruct((NE, bsz * s), F32)],
        compiler_params=_cparams("arbitrary", "arbitrary"),
        name="merge_outproj",
    )(attnt, ssmt, projt, x, mod, gb, wat, wst, wo, g2, rwt, rb)


def _route_body(lg_ref, e_ref, w_ref, r_ref, cnt_ref, carry_scr, *, tm):
    @pl.when(pl.program_id(0) == 0)
    def _():
        carry_scr[...] = jnp.zeros_like(carry_scr)

    work = lg_ref[...]
    eidx = lax.broadcasted_iota(I32, (NE, tm), 0).astype(F32)
    vals, idxs, hots = [], [], []
    for _ in range(TOPE):
        m = jnp.max(work, axis=0, keepdims=True)
        idx = jnp.min(jnp.where(work == m, eidx, float(NE)), axis=0, keepdims=True)
        hot = eidx == idx
        work = jnp.where(hot, NEG_INF, work)
        vals.append(m)
        idxs.append(idx)
        hots.append(hot)
    ex = [jnp.exp(v - vals[0]) for v in vals]
    den = ex[0] + ex[1] + ex[2] + ex[3]
    multi = jnp.zeros((NE, tm), F32)
    for hot in hots:
        multi = multi + jnp.where(hot, 1.0, 0.0)
    r_i = lax.broadcasted_iota(I32, (tm, tm), 0)
    c_i = lax.broadcasted_iota(I32, (tm, tm), 1)
    strict = jnp.where(r_i < c_i, 1.0, 0.0).astype(BF16)
    before = _dot(multi.astype(BF16), strict) + carry_scr[:, 0:1]
    for k in range(TOPE):
        e_ref[k:k + 1, :] = idxs[k].astype(I32)
        w_ref[k:k + 1, :] = ex[k] / den
        r_ref[k:k + 1, :] = jnp.sum(jnp.where(hots[k], before, 0.0), axis=0, keepdims=True).astype(I32)
    total = carry_scr[...] + jnp.sum(multi, axis=1, keepdims=True)
    carry_scr[...] = total
    cnt_ref[...] = total


def _route(logits_t, t0, th):
    tm = 512
    off = t0 // tm
    return pl.pallas_call(
        functools.partial(_route_body, tm=tm),
        grid=(th // tm,),
        in_specs=[pl.BlockSpec((NE, tm), lambda i: (0, off + i))],
        out_specs=[pl.BlockSpec((TOPE, tm), lambda i: (0, i)),
                   pl.BlockSpec((TOPE, tm), lambda i: (0, i)),
                   pl.BlockSpec((TOPE, tm), lambda i: (0, i)),
                   pl.BlockSpec((NE, 128), lambda i: (0, 0))],
        out_shape=[jax.ShapeDtypeStruct((TOPE, th), I32),
                   jax.ShapeDtypeStruct((TOPE, th), F32),
                   jax.ShapeDtypeStruct((TOPE, th), I32),
                   jax.ShapeDtypeStruct((NE, 128), F32)],
        scratch_shapes=[pltpu.VMEM((NE, 128), F32)],
        compiler_params=_cparams("arbitrary"),
        name="route_top4",
    )(logits_t)


class _Plan(NamedTuple):
    dest: jax.Array
    top_w: jax.Array
    p_end: jax.Array
    block_e: jax.Array
    n_used: jax.Array
    cap: int


def _plan(logits_t, t0, th):
    top_e, top_w, rank, cnt = _route(logits_t, t0, th)
    counts = cnt[:, 0].astype(I32)
    padded = (counts + MOE_ROWS - 1) // MOE_ROWS * MOE_ROWS
    p_end = jnp.cumsum(padded).astype(I32)
    p_start = p_end - padded
    experts = jnp.arange(NE, dtype=I32)
    dest = rank + jnp.sum(jnp.where(top_e[..., None] == experts, p_start, 0), axis=-1)
    cap = TOPE * th + NE * MOE_ROWS
    blk_start = jnp.arange(cap // MOE_ROWS, dtype=I32) * MOE_ROWS
    block_e = jnp.minimum(jnp.sum((p_end[None, :] <= blk_start[:, None]).astype(I32), axis=1), NE - 1)
    n_used = (p_end[-1] // MOE_ROWS).astype(I32).reshape(1)
    return _Plan(dest, top_w.T, p_end, block_e, n_used, cap)


def _dest_blocks(dest, tm):
    return dest.reshape(TOPE, dest.shape[1] // tm, tm).transpose(1, 0, 2)


def _row_copy(src, src_row, dst, dst_row, sem):
    return pltpu.make_async_copy(src.at[pl.ds(src_row, 1)], dst.at[pl.ds(dst_row, 1)], sem)


def _zero_fill(pe_ref, nu_ref, o_hbm, zero_scr, sem, nblk):
    zero_scr[...] = jnp.zeros_like(zero_scr)
    parts = MOE_ROWS // ZERO_ROWS

    def fill(blk, part):
        return pltpu.make_async_copy(zero_scr, o_hbm.at[pl.ds(blk * MOE_ROWS + part * ZERO_ROWS, ZERO_ROWS)], sem)

    pad_blocks = []
    for e in range(NE):
        prev_end = pe_ref[e - 1] if e > 0 else 0
        pad_blocks.append((pe_ref[e] > prev_end, pe_ref[e] // MOE_ROWS - 1))
    for b in range(NE):
        pad_blocks.append((nblk - 1 - b >= nu_ref[0], nblk - 1 - b))
    for cond, blk in pad_blocks:
        @pl.when(cond)
        def _(blk=blk):
            for part in range(parts):
                fill(blk, part).start()
    for cond, blk in pad_blocks:
        @pl.when(cond)
        def _(blk=blk):
            for part in range(parts):
                fill(blk, part).wait()


def _dispatch_body(pe_ref, nu_ref, dest_ref, u_ref, o_hbm, zero_scr, sem, *, nblk):
    @pl.when(pl.program_id(0) == 0)
    def _():
        _zero_fill(pe_ref, nu_ref, o_hbm, zero_scr, sem, nblk)

    def start(r, c):
        for k in range(TOPE):
            _row_copy(u_ref, r, o_hbm, dest_ref[0, k, r], sem).start()
        return c

    def wait(r, c):
        for k in range(TOPE):
            _row_copy(u_ref, 0, o_hbm, 0, sem).wait()
        return c

    lax.fori_loop(0, COMBINE_TOK, start, 0)
    lax.fori_loop(0, COMBINE_TOK, wait, 0)


def _dispatch(u2, plan, tile0):
    tm = COMBINE_TOK
    th = plan.dest.shape[1]
    nblk = plan.cap // MOE_ROWS
    grid_spec = pltpu.PrefetchScalarGridSpec(
        num_scalar_prefetch=2,
        grid=(th // tm,),
        in_specs=[pl.BlockSpec((1, TOPE, tm), lambda i, pe, nu: (i, 0, 0), memory_space=pltpu.SMEM),
                  pl.BlockSpec((tm, D), lambda i, pe, nu: (tile0 + i, 0))],
        out_specs=pl.BlockSpec(memory_space=pl.ANY),
        scratch_shapes=[pltpu.VMEM((ZERO_ROWS, D), F32), pltpu.SemaphoreType.DMA(())],
    )
    return pl.pallas_call(
        functools.partial(_dispatch_body, nblk=nblk),
        grid_spec=grid_spec,
        out_shape=jax.ShapeDtypeStruct((plan.cap, D), F32),
        compiler_params=_cparams("arbitrary"),
        name="moe_dispatch",
    )(plan.p_end, plan.n_used, _dest_blocks(plan.dest, tm), u2)


def _expert_body(*refs, nblk, with_dispatch, with_combine):
    refs = list(refs)

    def take(n):
        out = refs[:n]
        del refs[:n]
        return out

    be_ref, nu_ref = take(2)
    if with_dispatch:
        pe_ref, nud_ref = take(2)
    x_ref, w1_ref, b1_ref, w2_ref, b2_ref = take(5)
    if with_dispatch:
        ddest_ref, u_ref = take(2)
    if with_combine:
        cdest_ref, ysp_hbm, cw_ref, x1_ref, g2_ref = take(5)
    (o_ref,) = take(1)
    if with_dispatch:
        (xs_hbm,) = take(1)
    if with_combine:
        (out_ref,) = take(1)
    w1_scr, w2_scr = take(2)
    if with_dispatch:
        zero_scr, dsem = take(2)
    if with_combine:
        cbuf, csem = take(2)

    i = pl.program_id(0)
    changed = jnp.logical_or(i == 0, be_ref[i] != be_ref[jnp.maximum(i - 1, 0)])

    @pl.when(changed)
    def _():
        w1_scr[...] = w1_ref[0].astype(BF16)
        w2_scr[...] = w2_ref[0].astype(BF16)

    if with_dispatch:
        @pl.when(i == 0)
        def _():
            _zero_fill(pe_ref, nud_ref, xs_hbm, zero_scr, dsem, nblk)

    @pl.when(i < nu_ref[0])
    def _():
        if with_dispatch:
            for r in range(FUSE_TOK):
                for k in range(TOPE):
                    _row_copy(u_ref, r, xs_hbm, ddest_ref[0, k, r], dsem).start()
        if with_combine:
            for r in range(FUSE_TOK):
                for k in range(TOPE):
                    _row_copy(ysp_hbm, cdest_ref[0, k, r], cbuf.at[k], r, csem).start()
        h = _dot(x_ref[...].astype(BF16), w1_scr[...]) + b1_ref[0]
        glu = jnp.minimum(h[:, :FF], LIMIT)
        lin = jnp.clip(h[:, FF:], -LIMIT, LIMIT)
        act = glu * _sigmoid(ALPHA * glu) * (lin + 1.0)
        o_ref[...] = _dot(act.astype(BF16), w2_scr[...]) + b2_ref[0]
        if with_dispatch:
            def dwait(r, c):
                _row_copy(u_ref, 0, xs_hbm, 0, dsem).wait()
                return c
            lax.fori_loop(0, FUSE_TOK * TOPE, dwait, 0)
        if with_combine:
            def cwait(r, c):
                _row_copy(ysp_hbm, 0, cbuf.at[0], 0, csem).wait()
                return c
            lax.fori_loop(0, FUSE_TOK * TOPE, cwait, 0)
            w = cw_ref[...]
            ffn = cbuf[0] * w[:, 0:1]
            for k in range(1, TOPE):
                ffn = ffn + cbuf[k] * w[:, k:k + 1]
            out_ref[0] = x1_ref[0] + g2_ref[0] * ffn

    @pl.when(i >= nu_ref[0])
    def _():
        o_ref[...] = jnp.zeros_like(o_ref)


def _experts(xs, plan, w1, b1, w2, b2, dispatch=None, combine=None):
    cap = xs.shape[0]
    nblk = cap // MOE_ROWS
    ft = FUSE_TOK
    prefetch = [plan.block_e, plan.n_used]
    npf = 2 + (2 if dispatch else 0)
    in_specs = [pl.BlockSpec((MOE_ROWS, D), lambda i, *pf: (i, 0)),
                pl.BlockSpec((1, D, 2 * FF), lambda i, *pf: (pf[0][i], 0, 0)),
                pl.BlockSpec((1, 1, 2 * FF), lambda i, *pf: (pf[0][i], 0, 0)),
                pl.BlockSpec((1, FF, D), lambda i, *pf: (pf[0][i], 0, 0)),
                pl.BlockSpec((1, 1, D), lambda i, *pf: (pf[0][i], 0, 0))]
    args = [xs, w1, b1.reshape(NE, 1, 2 * FF), w2, b2.reshape(NE, 1, D)]
    out_specs = [pl.BlockSpec((MOE_ROWS, D), lambda i, *pf: (i, 0))]
    out_shape = [jax.ShapeDtypeStruct((cap, D), F32)]
    scratch = [pltpu.VMEM((D, 2 * FF), BF16), pltpu.VMEM((FF, D), BF16)]
    if dispatch:
        u2, dplan, tile0 = dispatch
        ntile = dplan.dest.shape[1] // ft
        assert ntile * ft * TOPE <= cap - NE * MOE_ROWS and dplan.cap == cap
        prefetch += [dplan.p_end, dplan.n_used]
        tile = lambda i: jnp.minimum(i, ntile - 1)
        in_specs += [pl.BlockSpec((1, TOPE, ft), lambda i, *pf: (tile(i), 0, 0), memory_space=pltpu.SMEM),
                     pl.BlockSpec((ft, D), lambda i, *pf: (tile0 + tile(i), 0))]
        args += [_dest_blocks(dplan.dest, ft), u2]
        out_specs += [pl.BlockSpec(memory_space=pl.ANY)]
        out_shape += [jax.ShapeDtypeStruct((cap, D), F32)]
        scratch += [pltpu.VMEM((ZERO_ROWS, D), F32), pltpu.SemaphoreType.DMA(())]
    if combine:
        ys_prev, cplan, x1, gate2, b0 = combine
        s = x1.shape[1]
        nm = s // ft
        ntile = cplan.dest.shape[1] // ft
        bh = ntile // nm
        tile = lambda i: jnp.minimum(i, ntile - 1)
        in_specs += [pl.BlockSpec((1, TOPE, ft), lambda i, *pf: (tile(i), 0, 0), memory_space=pltpu.SMEM),
                     pl.BlockSpec(memory_space=pl.ANY),
                     pl.BlockSpec((ft, TOPE), lambda i, *pf: (tile(i), 0)),
                     pl.BlockSpec((1, ft, D), lambda i, *pf: (b0 + tile(i) // nm, tile(i) % nm, 0)),
                     pl.BlockSpec((1, 1, D), lambda i, *pf: (b0 + tile(i) // nm, 0, 0))]
        args += [_dest_blocks(cplan.dest, ft), ys_prev, cplan.top_w, x1, gate2]
        out_specs += [pl.BlockSpec((1, ft, D), lambda i, *pf: (tile(i) // nm, tile(i) % nm, 0))]
        out_shape += [jax.ShapeDtypeStruct((bh, s, D), F32)]
        scratch += [pltpu.VMEM((TOPE, ft, D), F32), pltpu.SemaphoreType.DMA(())]
    grid_spec = pltpu.PrefetchScalarGridSpec(
        num_scalar_prefetch=npf, grid=(nblk,), in_specs=in_specs, out_specs=out_specs, scratch_shapes=scratch)
    return pl.pallas_call(
        functools.partial(_expert_body, nblk=nblk, with_dispatch=bool(dispatch), with_combine=bool(combine)),
        grid_spec=grid_spec,
        out_shape=out_shape,
        compiler_params=_cparams("arbitrary"),
        name="moe_experts",
    )(*prefetch, *args)


def _combine_body(dest_ref, ys_hbm, w_ref, x1_ref, g2_ref, o_ref, buf, sem):
    def start(r, c):
        for k in range(TOPE):
            _row_copy(ys_hbm, dest_ref[0, k, r], buf.at[k], r, sem).start()
        return c

    def wait(r, c):
        for k in range(TOPE):
            _row_copy(ys_hbm, 0, buf.at[k], 0, sem).wait()
        return c

    lax.fori_loop(0, COMBINE_TOK, start, 0)
    lax.fori_loop(0, COMBINE_TOK, wait, 0)
    w = w_ref[...]
    ffn = buf[0] * w[:, 0:1]
    for k in range(1, TOPE):
        ffn = ffn + buf[k] * w[:, k:k + 1]
    o_ref[0] = x1_ref[0] + g2_ref[0] * ffn


def _combine(ys, plan, x1, gate2, b0):
    s = x1.shape[1]
    tm = COMBINE_TOK
    nm = s // tm
    bh = plan.dest.shape[1] // s
    return pl.pallas_call(
        _combine_body,
        grid=(bh, nm),
        in_specs=[pl.BlockSpec((1, TOPE, tm), lambda b, m: (b * nm + m, 0, 0), memory_space=pltpu.SMEM),
                  pl.BlockSpec(memory_space=pl.ANY),
                  pl.BlockSpec((tm, TOPE), lambda b, m: (b * nm + m, 0)),
                  pl.BlockSpec((1, tm, D), lambda b, m: (b0 + b, m, 0)),
                  pl.BlockSpec((1, 1, D), lambda b, m: (b0 + b, 0, 0))],
        out_specs=pl.BlockSpec((1, tm, D), lambda b, m: (b, m, 0)),
        out_shape=jax.ShapeDtypeStruct((bh, s, D), F32),
        scratch_shapes=[pltpu.VMEM((TOPE, tm, D), F32), pltpu.SemaphoreType.DMA(())],
        compiler_params=_cparams("arbitrary", "arbitrary"),
        name="moe_combine",
    )(_dest_blocks(plan.dest, tm), ys, plan.top_w, x1, gate2)


def _layer(x, mod, norm1_g, w_in, q_norm_g, k_norm_g, rel_bias_table, conv_w, conv_b, dt_bias, a_log, d_skip,
           ssm_norm_g, w_attn_branch, w_ssm_branch, gate_bias, w_out, norm2_g, router_w, router_b,
           expert_w1, expert_b1, expert_w2, expert_b2):
    bsz, s, _ = x.shape
    t = bsz * s
    assert s % BLK == 0 and s % 512 == 0 and bsz % 2 == 0
    mod3 = mod.reshape(bsz, 6, D)

    c0 = 3 * ATTN_W
    c_x, c_b, c_dt, c_g = c0 + SSM_IN, c0 + SSM_IN + SSM_IN, c0 + SSM_IN + CONV_DIM, c0 + SSM_IN + CONV_DIM + SSM_H
    w_rows = jnp.concatenate([w_in[:, c0:c_x], w_in[:, c_g:], w_in[:, c_x:c_b], w_in[:, :c0], w_in[:, c_b:c_dt]],
                             axis=1)
    wt = w_rows.T.astype(BF16)
    wdt = w_in[:, c_dt:c_g].astype(BF16)
    projt, dtt, dts = _inproj(x, mod3, norm1_g.reshape(1, D), wt, wdt.T, wdt)

    bias = _bias_tiles(rel_bias_table.T.astype(F32), s // BLK)
    attnt = _attention(projt, bias, q_norm_g.reshape(HD, 1), k_norm_g.reshape(HD, 1))

    lanes = lambda v: jnp.broadcast_to(v[..., None], v.shape + (L,))
    ssmt = _ssd(projt, dtt, dts, lanes(conv_w), lanes(conv_b), dt_bias, a_log,
                lanes(jnp.repeat(d_skip, SSM_HD)), lanes(ssm_norm_g))

    x1, u2, logits_t = _merge(attnt, ssmt, projt, x, mod3, gate_bias.reshape(2 * D, 1),
                              w_attn_branch.T.astype(BF16), w_ssm_branch.T.astype(BF16), w_out.astype(BF16),
                              norm2_g.reshape(1, D), router_w.T.astype(BF16), router_b.reshape(NE, 1))

    bh = bsz // 2
    th = bh * s
    gate2 = mod3[:, 5:6, :]
    u2f = u2.reshape(t, D)
    ew = (expert_w1, expert_b1, expert_w2, expert_b2)
    plan_a = _plan(logits_t, 0, th)
    plan_b = _plan(logits_t, th, th)
    xs_a = _dispatch(u2f, plan_a, 0)
    ys_a, xs_b = _experts(xs_a, plan_a, *ew, dispatch=(u2f, plan_b, th // FUSE_TOK))
    ys_b, out_a = _experts(xs_b, plan_b, *ew, combine=(ys_a, plan_a, x1, gate2, 0))
    out_b = _combine(ys_b, plan_b, x1, gate2, bh)
    return jnp.concatenate([out_a, out_b], axis=0)


def kernel(x, c, ada_w, ada_b, norm1_g, w_in, q_norm_g, k_norm_g, rel_bias_table, conv_w, conv_b, dt_bias, a_log,
           d_skip, ssm_norm_g, w_attn_branch, w_ssm_branch, gate_bias, w_out, norm2_g, router_w, router_b,
           expert_w1, expert_b1, expert_w2, expert_b2):
    h = x
    for l in range(ada_w.shape[0]):
        mod = _ada(c, ada_w[l], ada_b[l])
        h = _layer(h, mod, norm1_g[l], w_in[l], q_norm_g[l], k_norm_g[l], rel_bias_table, conv_w[l], conv_b[l],
                   dt_bias[l], a_log[l], d_skip[l], ssm_norm_g[l], w_attn_branch[l], w_ssm_branch[l],
                   gate_bias[l], w_out[l], norm2_g[l], router_w[l], router_b[l], expert_w1[l], expert_b1[l],
                   expert_w2[l], expert_b2[l])
    return h
```

```python
import functools
import math
from typing import NamedTuple

import numpy as np
import jax
import jax.numpy as jnp
from jax import lax
from jax.experimental import pallas as pl
from jax.experimental.pallas import tpu as pltpu

F32, BF16, I32 = jnp.float32, jnp.bfloat16, jnp.int32

D = 1024
H = 16
HD = 64
ATTN_W = H * HD
SCALE = HD ** -0.5
BLK = 256
TOPK = 3
NBUCKET = 32
REL_EXACT = 16
REL_MAXD = 1024
SSM_IN = 2048
SSM_HD = 64
SSM_H = SSM_IN // SSM_HD
SSM_G = 4
SSM_J = SSM_H // SSM_G
SSM_N = 128
SSM_K = 4
L = 128
CONV_DIM = SSM_IN + 2 * SSM_G * SSM_N
NE = 32
TOPE = 4
FF = D
LIMIT = 7.0
ALPHA = 1.702
EPS = 1e-6
NEG_INF = float("-inf")

Z0, G0, X0, Q0, K0, V0, BC0 = 0, 2048, 4096, 6144, 7168, 8192, 9216
NPROJ = 10240

VA_ROWS = HD + 16
MOE_ROWS = 512
COMBINE_TOK = 256
ZERO_ROWS = 128
VMEM_LIMIT = 56 * 1024 * 1024


def _dot(a, b):
    return jnp.dot(a, b, preferred_element_type=F32)


def _dot_nt(a, b):
    return lax.dot_general(a, b, (((1,), (1,)), ((), ())), preferred_element_type=F32)


def _cparams(*sem):
    return pltpu.CompilerParams(dimension_semantics=sem, vmem_limit_bytes=VMEM_LIMIT)


def _sigmoid(x):
    return 0.5 * jnp.tanh(0.5 * x) + 0.5


def _softplus(x):
    return jnp.maximum(x, 0.0) + jnp.log1p(jnp.exp(-jnp.abs(x)))


def _ada_body(c_ref, w_ref, b_ref, o_ref):
    c = c_ref[...]
    ca = c * _sigmoid(c)
    o_ref[...] = _dot(ca.astype(BF16), w_ref[...].astype(BF16)) + b_ref[...]


def _ada(c, w, b):
    bsz = c.shape[0]
    n = w.shape[1]
    tn = 1536
    return pl.pallas_call(
        _ada_body,
        grid=(n // tn,),
        in_specs=[pl.BlockSpec((bsz, D), lambda j: (0, 0)),
                  pl.BlockSpec((D, tn), lambda j: (0, j)),
                  pl.BlockSpec((1, tn), lambda j: (0, j))],
        out_specs=pl.BlockSpec((bsz, tn), lambda j: (0, j)),
        out_shape=jax.ShapeDtypeStruct((bsz, n), F32),
        compiler_params=_cparams("arbitrary"),
        name="ada_mod",
    )(c, w, b.reshape(1, n))


def _inproj_body(x_ref, mod_ref, g_ref, w_ref, wdtt_ref, wdt_ref, o_ref, dtt_ref, dt_ref, u_scr):
    @pl.when(pl.program_id(2) == 0)
    def _():
        rows = 256
        for r in range(x_ref.shape[1] // rows):
            sl = slice(r * rows, (r + 1) * rows)
            x = x_ref[0, sl, :]
            ms = jnp.mean(x * x, axis=-1, keepdims=True)
            xn = x * lax.rsqrt(ms + EPS) * g_ref[...]
            u = xn * (1.0 + mod_ref[0, 1:2, :]) + mod_ref[0, 0:1, :]
            ub = u.astype(BF16)
            u_scr[sl, :] = ub
            dtt_ref[0, :, sl] = _dot_nt(wdtt_ref[...], ub)
            dt_ref[0, sl, :] = _dot(ub, wdt_ref[...])

    o_ref[0] = _dot_nt(w_ref[...], u_scr[...]).astype(BF16)


def _inproj(x, mod, g1, wt, wdtt, wdt):
    bsz, s, _ = x.shape
    tm = s
    tn = 1024
    return pl.pallas_call(
        _inproj_body,
        grid=(bsz, s // tm, NPROJ // tn),
        in_specs=[pl.BlockSpec((1, tm, D), lambda b, m, n: (b, m, 0)),
                  pl.BlockSpec((1, 6, D), lambda b, m, n: (b, 0, 0)),
                  pl.BlockSpec((1, D), lambda b, m, n: (0, 0)),
                  pl.BlockSpec((tn, D), lambda b, m, n: (n, 0)),
                  pl.BlockSpec((SSM_H, D), lambda b, m, n: (0, 0)),
                  pl.BlockSpec((D, SSM_H), lambda b, m, n: (0, 0))],
        out_specs=[pl.BlockSpec((1, tn, tm), lambda b, m, n: (b, n, m)),
                   pl.BlockSpec((1, SSM_H, tm), lambda b, m, n: (b, 0, m)),
                   pl.BlockSpec((1, tm, SSM_H), lambda b, m, n: (b, m, 0))],
        out_shape=[jax.ShapeDtypeStruct((bsz, NPROJ, s), BF16),
                   jax.ShapeDtypeStruct((bsz, SSM_H, s), F32),
                   jax.ShapeDtypeStruct((bsz, s, SSM_H), F32)],
        scratch_shapes=[pltpu.VMEM((tm, D), BF16)],
        compiler_params=_cparams("arbitrary", "arbitrary", "arbitrary"),
        name="in_proj",
    )(x, mod, g1, wt, wdtt, wdt)


def _t5_thresholds(max_dist):
    d = np.arange(max_dist, dtype=np.int64)
    df = np.maximum(d, 1).astype(np.float32)
    large = REL_EXACT + (np.log(df / np.float32(REL_EXACT)) / np.float32(math.log(REL_MAXD / REL_EXACT))
                         * np.float32(NBUCKET - REL_EXACT)).astype(np.int32)
    bucket = np.where(d < REL_EXACT, d, np.minimum(large, NBUCKET - 1))
    return [int(np.argmax(bucket >= b)) if np.any(bucket >= b) else int(max_dist) for b in range(NBUCKET)]


def _bias_body(tab_ref, o_ref, *, thr, nb):
    h = pl.program_id(0)
    ki = lax.broadcasted_iota(I32, (BLK, BLK), 0)
    qi = lax.broadcasted_iota(I32, (BLK, BLK), 1)
    for delta in range(nb):
        d = jnp.maximum(delta * BLK + qi - ki, 0)
        lo, hi = max(delta * BLK - BLK + 1, 0), delta * BLK + BLK - 1
        first = max(b for b in range(NBUCKET) if thr[b] <= lo)
        val = jnp.full((BLK, BLK), tab_ref[h, first], F32)
        for b in range(first + 1, NBUCKET):
            if thr[b] <= hi:
                val = jnp.where(d >= thr[b], tab_ref[h, b], val)
        if delta == 0:
            val = jnp.where(qi >= ki, val, NEG_INF)
        o_ref[0, delta] = val


def _bias_tiles(table_t, nb):
    thr = _t5_thresholds(nb * BLK + BLK)
    return pl.pallas_call(
        functools.partial(_bias_body, thr=thr, nb=nb),
        grid=(H,),
        in_specs=[pl.BlockSpec(memory_space=pltpu.SMEM)],
        out_specs=pl.BlockSpec((1, nb, BLK, BLK), lambda h: (h, 0, 0, 0)),
        out_shape=jax.ShapeDtypeStruct((H, nb, BLK, BLK), F32),
        compiler_params=_cparams("arbitrary"),
        name="t5_bias_tiles",
    )(table_t)


def _attn_body(q_ref, k_ref, v_ref, bias_ref, qg_ref, kg_ref, o_ref, k_scr, sc_scr, p_scr, va_scr, *, s):
    nb = s // BLK

    def normed(ref, g_ref, hh):
        t = ref[0, hh * HD:(hh + 1) * HD, :].astype(F32)
        ms = jnp.mean(t * t, axis=0, keepdims=True)
        return t * lax.rsqrt(ms + EPS) * g_ref[...]

    qn = [normed(q_ref, qg_ref, hh) for hh in range(2)]
    kn = [normed(k_ref, kg_ref, hh) for hh in range(2)]
    kpair = jnp.concatenate(kn, axis=0)
    for j in range(nb):
        k_scr[j] = kpair[:, j * BLK:(j + 1) * BLK].T.astype(BF16)
    zeros_q = jnp.zeros((HD, BLK), F32)
    row = lax.broadcasted_iota(I32, (VA_ROWS - HD, s), 0)
    ones_row = jnp.where(row == 0, 1.0, 0.0).astype(BF16)
    for hh in range(2):
        va_scr[hh, 0:HD, :] = v_ref[0, hh * HD:(hh + 1) * HD, :]
        va_scr[hh, HD:VA_ROWS, :] = ones_row

    for hh in range(2):
        first_q = min(TOPK + 1, nb) * BLK
        gate = []
        for j in range(nb - 1 if first_q < s else 0):
            kmean = jnp.mean(kn[hh][:, j * BLK:(j + 1) * BLK], axis=1, keepdims=True)
            gate.append(jnp.sum(qn[hh][:, first_q:] * kmean, axis=0, keepdims=True))
        qs = qn[hh] * SCALE
        for i in range(nb):
            sel = [None] * i
            if i > TOPK:
                rows = [g[:, i * BLK - first_q:(i + 1) * BLK - first_q] for g in gate[:i]]
                for j in range(i):
                    rank = jnp.zeros((1, BLK), F32)
                    for jj in range(i):
                        if jj != j:
                            beats = (rows[jj] >= rows[j]) if jj < j else (rows[jj] > rows[j])
                            rank = rank + jnp.where(beats, 1.0, 0.0)
                    sel[j] = jnp.where(rank < TOPK, 0.0, NEG_INF)
            qblk = qs[:, i * BLK:(i + 1) * BLK]
            qpad = jnp.concatenate([qblk, zeros_q] if hh == 0 else [zeros_q, qblk], axis=0).astype(BF16)
            m = None
            for j in range(i + 1):
                sc = _dot(k_scr[j], qpad) + bias_ref[hh, i - j]
                if j < i and sel[j] is not None:
                    sc = sc + sel[j]
                sc_scr[j] = sc
                mj = jnp.max(sc, axis=0, keepdims=True)
                m = mj if m is None else jnp.maximum(m, mj)
            for j in range(i + 1):
                p_scr[j * BLK:(j + 1) * BLK, :] = jnp.exp(sc_scr[j] - m).astype(BF16)
            acc = _dot(va_scr[hh, :, 0:(i + 1) * BLK], p_scr[0:(i + 1) * BLK, :])
            out = acc[0:HD] / acc[HD:HD + 1]
            o_ref[0, hh * HD:(hh + 1) * HD, i * BLK:(i + 1) * BLK] = out.astype(BF16)


def _attention(projt, bias, qg, kg):
    bsz, _, s = projt.shape
    nb = s // BLK
    rows = 2 * HD
    qb, kb, vb = Q0 // rows, K0 // rows, V0 // rows
    return pl.pallas_call(
        functools.partial(_attn_body, s=s),
        grid=(H // 2, bsz),
        in_specs=[pl.BlockSpec((1, rows, s), lambda p, b: (b, qb + p, 0)),
                  pl.BlockSpec((1, rows, s), lambda p, b: (b, kb + p, 0)),
                  pl.BlockSpec((1, rows, s), lambda p, b: (b, vb + p, 0)),
                  pl.BlockSpec((2, nb, BLK, BLK), lambda p, b: (p, 0, 0, 0)),
                  pl.BlockSpec((HD, 1), lambda p, b: (0, 0)),
                  pl.BlockSpec((HD, 1), lambda p, b: (0, 0))],
        out_specs=pl.BlockSpec((1, rows, s), lambda p, b: (b, p, 0)),
        out_shape=jax.ShapeDtypeStruct((bsz, ATTN_W, s), BF16),
        scratch_shapes=[pltpu.VMEM((nb, BLK, rows), BF16),
                        pltpu.VMEM((nb, BLK, BLK), F32),
                        pltpu.VMEM((s, BLK), BF16),
                        pltpu.VMEM((2, VA_ROWS, s), BF16)],
        compiler_params=_cparams("arbitrary", "arbitrary"),
        name="moba_attention",
    )(projt, projt, projt, bias, qg, kg)


def _split3(x):
    hi = x.astype(BF16)
    r1 = x - hi.astype(F32)
    mid = r1.astype(BF16)
    lo = (r1 - mid.astype(F32)).astype(BF16)
    return hi, mid, lo


def _ssd_body(z_ref, xs_ref, bc_ref, dtt_ref, dt_ref, cw_ref, cb_ref, dtbc_ref, dtbr_ref, alc_ref, alr_ref,
              dskip_ref, ng_ref, o_ref, prevx_scr, state_scr):
    @pl.when(pl.program_id(1) == 0)
    def _():
        prevx_scr[...] = jnp.zeros_like(prevx_scr)
        state_scr[...] = jnp.zeros_like(state_scr)

    cur = jnp.concatenate([xs_ref[0], bc_ref[0]], axis=0).astype(F32)
    prev = prevx_scr[...]
    lane = lax.broadcasted_iota(I32, (1, L), 1)
    acc = cb_ref[...] + cw_ref[SSM_K - 1] * cur
    for sft in range(1, SSM_K):
        shifted = pltpu.roll(jnp.where(lane >= L - sft, prev, cur), sft, 1)
        acc = acc + cw_ref[SSM_K - 1 - sft] * shifted
    prevx_scr[...] = cur
    xbc = acc * _sigmoid(acc)

    dtt = _softplus(dtt_ref[0] + dtbc_ref[...])
    dts = _softplus(dt_ref[0] + dtbr_ref[...])
    adtt = dtt * (-jnp.exp(alc_ref[...]))
    adts = dts * (-jnp.exp(alr_ref[...]))
    r_i = lax.broadcasted_iota(I32, (L, L), 0)
    c_i = lax.broadcasted_iota(I32, (L, L), 1)
    upper = jnp.where(r_i <= c_i, 1.0, 0.0).astype(BF16)
    lower = jnp.where(r_i >= c_i, 1.0, 0.0).astype(BF16)
    acst = sum(_dot(part, upper) for part in _split3(adtt))
    acs = sum(_dot(lower, part) for part in _split3(adts))
    causal = c_i >= r_i
    last = acst[:, L - 1:L]
    dte = jnp.exp(last - acst)
    cdec = jnp.broadcast_to(jnp.exp(last), (SSM_H, L))
    ein = jnp.exp(acst)

    for g in range(SSM_G):
        bt = xbc[SSM_IN + g * SSM_N:SSM_IN + (g + 1) * SSM_N]
        ct = xbc[SSM_IN + SSM_G * SSM_N + g * SSM_N:SSM_IN + SSM_G * SSM_N + (g + 1) * SSM_N]
        ctb = ct.astype(BF16)
        bmb = bt.T.astype(BF16)
        cbt = _dot(bmb, ctb)
        rows = slice(g * SSM_J * SSM_HD, (g + 1) * SSM_J * SSM_HD)
        xg = xbc[rows]
        state = state_scr[rows]
        yoff = _dot(state.astype(BF16), ctb)
        xdt, sin, cd, ys = [], [], [], []
        for j in range(SSM_J):
            h = g * SSM_J + j
            xj = xg[j * SSM_HD:(j + 1) * SSM_HD]
            xdj = xj * dtt[h:h + 1]
            xdt.append(xdj)
            sin.append(xdj * dte[h:h + 1])
            cd.append(jnp.broadcast_to(cdec[h:h + 1], (SSM_HD, SSM_N)))
        new_state = _dot(jnp.concatenate(sin, axis=0).astype(BF16), bmb)
        state_scr[rows] = state * jnp.concatenate(cd, axis=0) + new_state
        for j in range(SSM_J):
            h = g * SSM_J + j
            seg = acst[h:h + 1] - acs[:, h:h + 1]
            mt = cbt * jnp.exp(jnp.where(causal, seg, NEG_INF))
            ydiag = _dot(xdt[j].astype(BF16), mt.astype(BF16))
            ys.append(ydiag + yoff[j * SSM_HD:(j + 1) * SSM_HD] * ein[h:h + 1])
        y = jnp.concatenate(ys, axis=0) + dskip_ref[rows] * xg
        zg = z_ref[0, rows].astype(F32)
        y = y * (zg * _sigmoid(zg))
        ms = jnp.mean(y * y, axis=0, keepdims=True)
        o_ref[0, rows] = (y * lax.rsqrt(ms + EPS) * ng_ref[rows]).astype(BF16)


def _ssd(projt, dtt, dts, cw, cb, dtb, alog, dskip, ng):
    bsz, _, s = projt.shape
    nc = s // L
    full = lambda shape: pl.BlockSpec(shape, lambda b, c: (0,) * len(shape))
    return pl.pallas_call(
        _ssd_body,
        grid=(bsz, nc),
        in_specs=[pl.BlockSpec((1, SSM_IN, L), lambda b, c: (b, Z0 // SSM_IN, c)),
                  pl.BlockSpec((1, SSM_IN, L), lambda b, c: (b, X0 // SSM_IN, c)),
                  pl.BlockSpec((1, 2 * SSM_G * SSM_N, L), lambda b, c: (b, BC0 // (2 * SSM_G * SSM_N), c)),
                  pl.BlockSpec((1, SSM_H, L), lambda b, c: (b, 0, c)),
                  pl.BlockSpec((1, L, SSM_H), lambda b, c: (b, c, 0)),
                  full((SSM_K, CONV_DIM, L)), full((CONV_DIM, L)),
                  full((SSM_H, 1)), full((1, SSM_H)), full((SSM_H, 1)), full((1, SSM_H)),
                  full((SSM_IN, L)), full((SSM_IN, L))],
        out_specs=pl.BlockSpec((1, SSM_IN, L), lambda b, c: (b, 0, c)),
        out_shape=jax.ShapeDtypeStruct((bsz, SSM_IN, s), BF16),
        scratch_shapes=[pltpu.VMEM((CONV_DIM, L), F32), pltpu.VMEM((SSM_IN, SSM_N), F32)],
        compiler_params=_cparams("arbitrary", "arbitrary"),
        name="ssd_branch",
    )(projt, projt, projt, dtt, dts, cw, cb, dtb.reshape(SSM_H, 1), dtb.reshape(1, SSM_H),
      alog.reshape(SSM_H, 1), alog.reshape(1, SSM_H), dskip, ng)


def _merge_body(attn_ref, ssm_ref, gate_ref, x_ref, mod_ref, gb_ref, wat_ref, wst_ref, wo_ref, g2_ref,
                rwt_ref, rb_ref, x1_ref, u2_ref, lg_ref):
    a = _dot(wat_ref[...], attn_ref[0])
    sm = _dot(wst_ref[...], ssm_ref[0])
    gate = _sigmoid(gate_ref[0].astype(F32) + gb_ref[...])
    merged_t = gate[:D] * a + gate[D:] * sm
    merged = merged_t.T.astype(BF16)
    y = _dot(merged, wo_ref[...])
    x1 = x_ref[0] + mod_ref[0, 2:3, :] * y
    x1_ref[0] = x1
    ms = jnp.mean(x1 * x1, axis=-1, keepdims=True)
    u2 = (x1 * lax.rsqrt(ms + EPS) * g2_ref[...]) * (1.0 + mod_ref[0, 4:5, :]) + mod_ref[0, 3:4, :]
    u2_ref[0] = u2
    lg_ref[...] = _dot_nt(rwt_ref[...], u2.astype(BF16)) + rb_ref[...]


def _merge(attnt, ssmt, projt, x, mod, gb, wat, wst, wo, g2, rwt, rb):
    bsz, s, _ = x.shape
    tm = 512
    nm = s // tm
    full = lambda shape: pl.BlockSpec(shape, lambda b, m: (0,) * len(shape))
    return pl.pallas_call(
        _merge_body,
        grid=(bsz, nm),
        in_specs=[pl.BlockSpec((1, ATTN_W, tm), lambda b, m: (b, 0, m)),
                  pl.BlockSpec((1, SSM_IN, tm), lambda b, m: (b, 0, m)),
                  pl.BlockSpec((1, 2 * D, tm), lambda b, m: (b, G0 // (2 * D), m)),
                  pl.BlockSpec((1, tm, D), lambda b, m: (b, m, 0)),
                  pl.BlockSpec((1, 6, D), lambda b, m: (b, 0, 0)),
                  full((2 * D, 1)), full((D, ATTN_W)), full((D, SSM_IN)), full((D, D)), full((1, D)),
                  full((NE, D)), full((NE, 1))],
        out_specs=[pl.BlockSpec((1, tm, D), lambda b, m: (b, m, 0)),
                   pl.BlockSpec((1, tm, D), lambda b, m: (b, m, 0)),
                   pl.BlockSpec((NE, tm), lambda b, m: (0, b * nm + m))],
        out_shape=[jax.ShapeDtypeStruct((bsz, s, D), F32),
                   jax.ShapeDtypeStruct((bsz, s, D), F32),
                   jax.ShapeDtypeStruct((NE, bsz * s), F32)],
        compiler_params=_cparams("arbitrary", "arbitrary"),
        name="merge_outproj",
    )(attnt, ssmt, projt, x, mod, gb, wat, wst, wo, g2, rwt, rb)


def _route_body(lg_ref, e_ref, w_ref, r_ref, cnt_ref, carry_scr, *, tm):
    @pl.when(pl.program_id(0) == 0)
    def _():
        carry_scr[...] = jnp.zeros_like(carry_scr)

    work = lg_ref[...]
    eidx = lax.broadcasted_iota(I32, (NE, tm), 0).astype(F32)
    vals, idxs, hots = [], [], []
    for _ in range(TOPE):
        m = jnp.max(work, axis=0, keepdims=True)
        idx = jnp.min(jnp.where(work == m, eidx, float(NE)), axis=0, keepdims=True)
        hot = eidx == idx
        work = jnp.where(hot, NEG_INF, work)
        vals.append(m)
        idxs.append(idx)
        hots.append(hot)
    ex = [jnp.exp(v - vals[0]) for v in vals]
    den = ex[0] + ex[1] + ex[2] + ex[3]
    multi = jnp.zeros((NE, tm), F32)
    for hot in hots:
        multi = multi + jnp.where(hot, 1.0, 0.0)
    r_i = lax.broadcasted_iota(I32, (tm, tm), 0)
    c_i = lax.broadcasted_iota(I32, (tm, tm), 1)
    strict = jnp.where(r_i < c_i, 1.0, 0.0).astype(BF16)
    before = _dot(multi.astype(BF16), strict) + carry_scr[:, 0:1]
    for k in range(TOPE):
        e_ref[k:k + 1, :] = idxs[k].astype(I32)
        w_ref[k:k + 1, :] = ex[k] / den
        r_ref[k:k + 1, :] = jnp.sum(jnp.where(hots[k], before, 0.0), axis=0, keepdims=True).astype(I32)
    total = carry_scr[...] + jnp.sum(multi, axis=1, keepdims=True)
    carry_scr[...] = total
    cnt_ref[...] = total


def _route(logits_t, t0, th):
    tm = 512
    off = t0 // tm
    return pl.pallas_call(
        functools.partial(_route_body, tm=tm),
        grid=(th // tm,),
        in_specs=[pl.BlockSpec((NE, tm), lambda i: (0, off + i))],
        out_specs=[pl.BlockSpec((TOPE, tm), lambda i: (0, i)),
                   pl.BlockSpec((TOPE, tm), lambda i: (0, i)),
                   pl.BlockSpec((TOPE, tm), lambda i: (0, i)),
                   pl.BlockSpec((NE, 128), lambda i: (0, 0))],
        out_shape=[jax.ShapeDtypeStruct((TOPE, th), I32),
                   jax.ShapeDtypeStruct((TOPE, th), F32),
                   jax.ShapeDtypeStruct((TOPE, th), I32),
                   jax.ShapeDtypeStruct((NE, 128), F32)],
        scratch_shapes=[pltpu.VMEM((NE, 128), F32)],
        compiler_params=_cparams("arbitrary"),
        name="route_top4",
    )(logits_t)


class _Plan(NamedTuple):
    dest: jax.Array
    top_w: jax.Array
    p_end: jax.Array
    block_e: jax.Array
    n_used: jax.Array
    cap: int


def _plan(logits_t, t0, th):
    top_e, top_w, rank, cnt = _route(logits_t, t0, th)
    counts = cnt[:, 0].astype(I32)
    padded = (counts + MOE_ROWS - 1) // MOE_ROWS * MOE_ROWS
    p_end = jnp.cumsum(padded).astype(I32)
    p_start = p_end - padded
    experts = jnp.arange(NE, dtype=I32)
    dest = rank + jnp.sum(jnp.where(top_e[..., None] == experts, p_start, 0), axis=-1)
    cap = TOPE * th + NE * MOE_ROWS
    blk_start = jnp.arange(cap // MOE_ROWS, dtype=I32) * MOE_ROWS
    block_e = jnp.minimum(jnp.sum((p_end[None, :] <= blk_start[:, None]).astype(I32), axis=1), NE - 1)
    n_used = (p_end[-1] // MOE_ROWS).astype(I32).reshape(1)
    return _Plan(dest, top_w.T, p_end, block_e, n_used, cap)


def _dest_blocks(dest, tm):
    return dest.reshape(TOPE, dest.shape[1] // tm, tm).transpose(1, 0, 2)


def _row_copy(src, src_row, dst, dst_row, sem):
    return pltpu.make_async_copy(src.at[pl.ds(src_row, 1)], dst.at[pl.ds(dst_row, 1)], sem)


def _zero_fill(pe_ref, nu_ref, o_hbm, zero_scr, sem, nblk):
    zero_scr[...] = jnp.zeros_like(zero_scr)
    parts = MOE_ROWS // ZERO_ROWS

    def fill(blk, part):
        return pltpu.make_async_copy(zero_scr, o_hbm.at[pl.ds(blk * MOE_ROWS + part * ZERO_ROWS, ZERO_ROWS)], sem)

    pad_blocks = []
    for e in range(NE):
        prev_end = pe_ref[e - 1] if e > 0 else 0
        pad_blocks.append((pe_ref[e] > prev_end, pe_ref[e] // MOE_ROWS - 1))
    for b in range(NE):
        pad_blocks.append((nblk - 1 - b >= nu_ref[0], nblk - 1 - b))
    for cond, blk in pad_blocks:
        @pl.when(cond)
        def _(blk=blk):
            for part in range(parts):
                fill(blk, part).start()
    for cond, blk in pad_blocks:
        @pl.when(cond)
        def _(blk=blk):
            for part in range(parts):
                fill(blk, part).wait()


def _dispatch_body(pe_ref, nu_ref, dest_ref, u_ref, o_hbm, zero_scr, sem, *, nblk):
    @pl.when(pl.program_id(0) == 0)
    def _():
        _zero_fill(pe_ref, nu_ref, o_hbm, zero_scr, sem, nblk)

    def start(r, c):
        for k in range(TOPE):
            _row_copy(u_ref, r, o_hbm, dest_ref[0, k, r], sem).start()
        return c

    def wait(r, c):
        for k in range(TOPE):
            _row_copy(u_ref, 0, o_hbm, 0, sem).wait()
        return c

    lax.fori_loop(0, COMBINE_TOK, start, 0)
    lax.fori_loop(0, COMBINE_TOK, wait, 0)


def _dispatch(u2, plan, tile0):
    tm = COMBINE_TOK
    th = plan.dest.shape[1]
    nblk = plan.cap // MOE_ROWS
    grid_spec = pltpu.PrefetchScalarGridSpec(
        num_scalar_prefetch=2,
        grid=(th // tm,),
        in_specs=[pl.BlockSpec((1, TOPE, tm), lambda i, pe, nu: (i, 0, 0), memory_space=pltpu.SMEM),
                  pl.BlockSpec((tm, D), lambda i, pe, nu: (tile0 + i, 0))],
        out_specs=pl.BlockSpec(memory_space=pl.ANY),
        scratch_shapes=[pltpu.VMEM((ZERO_ROWS, D), F32), pltpu.SemaphoreType.DMA(())],
    )
    return pl.pallas_call(
        functools.partial(_dispatch_body, nblk=nblk),
        grid_spec=grid_spec,
        out_shape=jax.ShapeDtypeStruct((plan.cap, D), F32),
        compiler_params=_cparams("arbitrary"),
        name="moe_dispatch",
    )(plan.p_end, plan.n_used, _dest_blocks(plan.dest, tm), u2)


def _expert_body(be_ref, nu_ref, x_ref, w1_ref, b1_ref, w2_ref, b2_ref, o_ref, w1_scr, w2_scr):
    i = pl.program_id(0)
    changed = jnp.logical_or(i == 0, be_ref[i] != be_ref[jnp.maximum(i - 1, 0)])

    @pl.when(changed)
    def _():
        w1_scr[...] = w1_ref[0].astype(BF16)
        w2_scr[...] = w2_ref[0].astype(BF16)

    @pl.when(i < nu_ref[0])
    def _():
        h = _dot(x_ref[...].astype(BF16), w1_scr[...]) + b1_ref[0]
        glu = jnp.minimum(h[:, :FF], LIMIT)
        lin = jnp.clip(h[:, FF:], -LIMIT, LIMIT)
        act = glu * _sigmoid(ALPHA * glu) * (lin + 1.0)
        o_ref[...] = _dot(act.astype(BF16), w2_scr[...]) + b2_ref[0]

    @pl.when(i >= nu_ref[0])
    def _():
        o_ref[...] = jnp.zeros_like(o_ref)


def _experts(xs, plan, w1, b1, w2, b2):
    cap = xs.shape[0]
    nblk = cap // MOE_ROWS
    grid_spec = pltpu.PrefetchScalarGridSpec(
        num_scalar_prefetch=2,
        grid=(nblk,),
        in_specs=[pl.BlockSpec((MOE_ROWS, D), lambda i, be, nu: (i, 0)),
                  pl.BlockSpec((1, D, 2 * FF), lambda i, be, nu: (be[i], 0, 0)),
                  pl.BlockSpec((1, 1, 2 * FF), lambda i, be, nu: (be[i], 0, 0)),
                  pl.BlockSpec((1, FF, D), lambda i, be, nu: (be[i], 0, 0)),
                  pl.BlockSpec((1, 1, D), lambda i, be, nu: (be[i], 0, 0))],
        out_specs=pl.BlockSpec((MOE_ROWS, D), lambda i, be, nu: (i, 0)),
        scratch_shapes=[pltpu.VMEM((D, 2 * FF), BF16), pltpu.VMEM((FF, D), BF16)],
    )
    return pl.pallas_call(
        _expert_body,
        grid_spec=grid_spec,
        out_shape=jax.ShapeDtypeStruct((cap, D), F32),
        compiler_params=_cparams("arbitrary"),
        name="moe_experts",
    )(plan.block_e, plan.n_used, xs, w1, b1.reshape(NE, 1, 2 * FF), w2, b2.reshape(NE, 1, D))


def _combine_body(dest_ref, ys_hbm, w_ref, x1_ref, g2_ref, o_ref, buf, sem):
    def start(r, c):
        for k in range(TOPE):
            _row_copy(ys_hbm, dest_ref[0, k, r], buf.at[k], r, sem).start()
        return c

    def wait(r, c):
        for k in range(TOPE):
            _row_copy(ys_hbm, 0, buf.at[k], 0, sem).wait()
        return c

    lax.fori_loop(0, COMBINE_TOK, start, 0)
    lax.fori_loop(0, COMBINE_TOK, wait, 0)
    w = w_ref[...]
    ffn = buf[0] * w[:, 0:1]
    for k in range(1, TOPE):
        ffn = ffn + buf[k] * w[:, k:k + 1]
    o_ref[0] = x1_ref[0] + g2_ref[0] * ffn


def _combine(ys, plan, x1, gate2, b0):
    s = x1.shape[1]
    tm = COMBINE_TOK
    nm = s // tm
    bh = plan.dest.shape[1] // s
    return pl.pallas_call(
        _combine_body,
        grid=(bh, nm),
        in_specs=[pl.BlockSpec((1, TOPE, tm), lambda b, m: (b * nm + m, 0, 0), memory_space=pltpu.SMEM),
                  pl.BlockSpec(memory_space=pl.ANY),
                  pl.BlockSpec((tm, TOPE), lambda b, m: (b * nm + m, 0)),
                  pl.BlockSpec((1, tm, D), lambda b, m: (b0 + b, m, 0)),
                  pl.BlockSpec((1, 1, D), lambda b, m: (b0 + b, 0, 0))],
        out_specs=pl.BlockSpec((1, tm, D), lambda b, m: (b, m, 0)),
        out_shape=jax.ShapeDtypeStruct((bh, s, D), F32),
        scratch_shapes=[pltpu.VMEM((TOPE, tm, D), F32), pltpu.SemaphoreType.DMA(())],
        compiler_params=_cparams("arbitrary", "arbitrary"),
        name="moe_combine",
    )(_dest_blocks(plan.dest, tm), ys, plan.top_w, x1, gate2)


def _layer(x, mod, norm1_g, w_in, q_norm_g, k_norm_g, rel_bias_table, conv_w, conv_b, dt_bias, a_log, d_skip,
           ssm_norm_g, w_attn_branch, w_ssm_branch, gate_bias, w_out, norm2_g, router_w, router_b,
           expert_w1, expert_b1, expert_w2, expert_b2):
    bsz, s, _ = x.shape
    t = bsz * s
    assert s % BLK == 0 and s % 512 == 0
    mod3 = mod.reshape(bsz, 6, D)

    c0 = 3 * ATTN_W
    c_x, c_b, c_dt, c_g = c0 + SSM_IN, c0 + SSM_IN + SSM_IN, c0 + SSM_IN + CONV_DIM, c0 + SSM_IN + CONV_DIM + SSM_H
    w_rows = jnp.concatenate([w_in[:, c0:c_x], w_in[:, c_g:], w_in[:, c_x:c_b], w_in[:, :c0], w_in[:, c_b:c_dt]],
                             axis=1)
    wt = w_rows.T.astype(BF16)
    wdt = w_in[:, c_dt:c_g].astype(BF16)
    projt, dtt, dts = _inproj(x, mod3, norm1_g.reshape(1, D), wt, wdt.T, wdt)

    bias = _bias_tiles(rel_bias_table.T.astype(F32), s // BLK)
    attnt = _attention(projt, bias, q_norm_g.reshape(HD, 1), k_norm_g.reshape(HD, 1))

    lanes = lambda v: jnp.broadcast_to(v[..., None], v.shape + (L,))
    ssmt = _ssd(projt, dtt, dts, lanes(conv_w), lanes(conv_b), dt_bias, a_log,
                lanes(jnp.repeat(d_skip, SSM_HD)), lanes(ssm_norm_g))

    x1, u2, logits_t = _merge(attnt, ssmt, projt, x, mod3, gate_bias.reshape(2 * D, 1),
                              w_attn_branch.T.astype(BF16), w_ssm_branch.T.astype(BF16), w_out.astype(BF16),
                              norm2_g.reshape(1, D), router_w.T.astype(BF16), router_b.reshape(NE, 1))

    plan = _plan(logits_t, 0, t)
    xs = _dispatch(u2.reshape(t, D), plan, 0)
    ys = _experts(xs, plan, expert_w1, expert_b1, expert_w2, expert_b2)
    return _combine(ys, plan, x1, mod3[:, 5:6, :], 0)


def kernel(x, c, ada_w, ada_b, norm1_g, w_in, q_norm_g, k_norm_g, rel_bias_table, conv_w, conv_b, dt_bias, a_log,
           d_skip, ssm_norm_g, w_attn_branch, w_ssm_branch, gate_bias, w_out, norm2_g, router_w, router_b,
           expert_w1, expert_b1, expert_w2, expert_b2):
    h = x
    for l in range(ada_w.shape[0]):
        mod = _ada(c, ada_w[l], ada_b[l])
        h = _layer(h, mod, norm1_g[l], w_in[l], q_norm_g[l], k_norm_g[l], rel_bias_table, conv_w[l], conv_b[l],
                   dt_bias[l], a_log[l], d_skip[l], ssm_norm_g[l], w_attn_branch[l], w_ssm_branch[l],
                   gate_bias[l], w_out[l], norm2_g[l], router_w[l], router_b[l], expert_w1[l], expert_b1[l],
                   expert_w2[l], expert_b2[l])
    return h
```

```python
import functools
import math
from typing import NamedTuple

import numpy as np
import jax
import jax.numpy as jnp
from jax import lax
from jax.experimental import pallas as pl
from jax.experimental.pallas import tpu as pltpu

F32, BF16, I32 = jnp.float32, jnp.bfloat16, jnp.int32

D = 1024
H = 16
HD = 64
ATTN_W = H * HD
SCALE = HD ** -0.5
BLK = 256
TOPK = 3
NBUCKET = 32
REL_EXACT = 16
REL_MAXD = 1024
SSM_IN = 2048
SSM_HD = 64
SSM_H = SSM_IN // SSM_HD
SSM_G = 4
SSM_J = SSM_H // SSM_G
SSM_N = 128
SSM_K = 4
L = 128
CONV_DIM = SSM_IN + 2 * SSM_G * SSM_N
NE = 32
TOPE = 4
FF = D
LIMIT = 7.0
ALPHA = 1.702
EPS = 1e-6
NEG_INF = float("-inf")

Z0, G0, X0, Q0, K0, V0, BC0 = 0, 2048, 4096, 6144, 7168, 8192, 9216
NPROJ = 10240

VA_ROWS = HD + 16
MOE_ROWS = 512
COMBINE_TOK = 512
ZERO_ROWS = 128
VMEM_LIMIT = 56 * 1024 * 1024


def _dot(a, b):
    return jnp.dot(a, b, preferred_element_type=F32)


def _dot_nt(a, b):
    return lax.dot_general(a, b, (((1,), (1,)), ((), ())), preferred_element_type=F32)


def _cparams(*sem):
    return pltpu.CompilerParams(dimension_semantics=sem, vmem_limit_bytes=VMEM_LIMIT)


def _sigmoid(x):
    return 0.5 * jnp.tanh(0.5 * x) + 0.5


def _softplus(x):
    return jnp.maximum(x, 0.0) + jnp.log1p(jnp.exp(-jnp.abs(x)))


def _ada_body(c_ref, w_ref, b_ref, o_ref):
    c = c_ref[...]
    ca = c * _sigmoid(c)
    o_ref[...] = _dot(ca.astype(BF16), w_ref[...].astype(BF16)) + b_ref[...]


def _ada(c, w, b):
    bsz = c.shape[0]
    n = w.shape[1]
    tn = 1536
    return pl.pallas_call(
        _ada_body,
        grid=(n // tn,),
        in_specs=[pl.BlockSpec((bsz, D), lambda j: (0, 0)),
                  pl.BlockSpec((D, tn), lambda j: (0, j)),
                  pl.BlockSpec((1, tn), lambda j: (0, j))],
        out_specs=pl.BlockSpec((bsz, tn), lambda j: (0, j)),
        out_shape=jax.ShapeDtypeStruct((bsz, n), F32),
        compiler_params=_cparams("arbitrary"),
        name="ada_mod",
    )(c, w, b.reshape(1, n))


def _inproj_body(x_ref, mod_ref, g_ref, w_ref, wdtt_ref, wdt_ref, o_ref, dtt_ref, dt_ref, u_scr):
    @pl.when(pl.program_id(2) == 0)
    def _():
        rows = 256
        for r in range(x_ref.shape[1] // rows):
            sl = slice(r * rows, (r + 1) * rows)
            x = x_ref[0, sl, :]
            ms = jnp.mean(x * x, axis=-1, keepdims=True)
            xn = x * lax.rsqrt(ms + EPS) * g_ref[...]
            u = xn * (1.0 + mod_ref[0, 1:2, :]) + mod_ref[0, 0:1, :]
            ub = u.astype(BF16)
            u_scr[sl, :] = ub
            dtt_ref[0, :, sl] = _dot_nt(wdtt_ref[...], ub)
            dt_ref[0, sl, :] = _dot(ub, wdt_ref[...])

    o_ref[0] = _dot_nt(w_ref[...], u_scr[...]).astype(BF16)


def _inproj(x, mod, g1, wt, wdtt, wdt):
    bsz, s, _ = x.shape
    tm = s
    tn = 1024
    return pl.pallas_call(
        _inproj_body,
        grid=(bsz, s // tm, NPROJ // tn),
        in_specs=[pl.BlockSpec((1, tm, D), lambda b, m, n: (b, m, 0)),
                  pl.BlockSpec((1, 6, D), lambda b, m, n: (b, 0, 0)),
                  pl.BlockSpec((1, D), lambda b, m, n: (0, 0)),
                  pl.BlockSpec((tn, D), lambda b, m, n: (n, 0)),
                  pl.BlockSpec((SSM_H, D), lambda b, m, n: (0, 0)),
                  pl.BlockSpec((D, SSM_H), lambda b, m, n: (0, 0))],
        out_specs=[pl.BlockSpec((1, tn, tm), lambda b, m, n: (b, n, m)),
                   pl.BlockSpec((1, SSM_H, tm), lambda b, m, n: (b, 0, m)),
                   pl.BlockSpec((1, tm, SSM_H), lambda b, m, n: (b, m, 0))],
        out_shape=[jax.ShapeDtypeStruct((bsz, NPROJ, s), BF16),
                   jax.ShapeDtypeStruct((bsz, SSM_H, s), F32),
                   jax.ShapeDtypeStruct((bsz, s, SSM_H), F32)],
        scratch_shapes=[pltpu.VMEM((tm, D), BF16)],
        compiler_params=_cparams("arbitrary", "arbitrary", "arbitrary"),
        name="in_proj",
    )(x, mod, g1, wt, wdtt, wdt)


def _t5_thresholds(max_dist):
    d = np.arange(max_dist, dtype=np.int64)
    df = np.maximum(d, 1).astype(np.float32)
    large = REL_EXACT + (np.log(df / np.float32(REL_EXACT)) / np.float32(math.log(REL_MAXD / REL_EXACT))
                         * np.float32(NBUCKET - REL_EXACT)).astype(np.int32)
    bucket = np.where(d < REL_EXACT, d, np.minimum(large, NBUCKET - 1))
    return [int(np.argmax(bucket >= b)) if np.any(bucket >= b) else int(max_dist) for b in range(NBUCKET)]


def _bias_body(tab_ref, o_ref, *, thr, nb):
    h = pl.program_id(0)
    ki = lax.broadcasted_iota(I32, (BLK, BLK), 0)
    qi = lax.broadcasted_iota(I32, (BLK, BLK), 1)
    for delta in range(nb):
        d = jnp.maximum(delta * BLK + qi - ki, 0)
        lo, hi = max(delta * BLK - BLK + 1, 0), delta * BLK + BLK - 1
        first = max(b for b in range(NBUCKET) if thr[b] <= lo)
        val = jnp.full((BLK, BLK), tab_ref[h, first], F32)
        for b in range(first + 1, NBUCKET):
            if thr[b] <= hi:
                val = jnp.where(d >= thr[b], tab_ref[h, b], val)
        if delta == 0:
            val = jnp.where(qi >= ki, val, NEG_INF)
        o_ref[0, delta] = val


def _bias_tiles(table_t, nb):
    thr = _t5_thresholds(nb * BLK + BLK)
    return pl.pallas_call(
        functools.partial(_bias_body, thr=thr, nb=nb),
        grid=(H,),
        in_specs=[pl.BlockSpec(memory_space=pltpu.SMEM)],
        out_specs=pl.BlockSpec((1, nb, BLK, BLK), lambda h: (h, 0, 0, 0)),
        out_shape=jax.ShapeDtypeStruct((H, nb, BLK, BLK), F32),
        compiler_params=_cparams("arbitrary"),
        name="t5_bias_tiles",
    )(table_t)


def _attn_body(q_ref, k_ref, v_ref, bias_ref, qg_ref, kg_ref, o_ref, k_scr, sc_scr, p_scr, va_scr, *, s):
    nb = s // BLK

    def normed(ref, g_ref, hh):
        t = ref[0, hh * HD:(hh + 1) * HD, :].astype(F32)
        ms = jnp.mean(t * t, axis=0, keepdims=True)
        return t * lax.rsqrt(ms + EPS) * g_ref[...]

    qn = [normed(q_ref, qg_ref, hh) for hh in range(2)]
    kn = [normed(k_ref, kg_ref, hh) for hh in range(2)]
    kpair = jnp.concatenate(kn, axis=0)
    for j in range(nb):
        k_scr[j] = kpair[:, j * BLK:(j + 1) * BLK].T.astype(BF16)
    zeros_q = jnp.zeros((HD, BLK), F32)
    row = lax.broadcasted_iota(I32, (VA_ROWS - HD, s), 0)
    ones_row = jnp.where(row == 0, 1.0, 0.0).astype(BF16)
    for hh in range(2):
        va_scr[hh, 0:HD, :] = v_ref[0, hh * HD:(hh + 1) * HD, :]
        va_scr[hh, HD:VA_ROWS, :] = ones_row

    for hh in range(2):
        first_q = min(TOPK + 1, nb) * BLK
        gate = []
        for j in range(nb - 1 if first_q < s else 0):
            kmean = jnp.mean(kn[hh][:, j * BLK:(j + 1) * BLK], axis=1, keepdims=True)
            gate.append(jnp.sum(qn[hh][:, first_q:] * kmean, axis=0, keepdims=True))
        qs = qn[hh] * SCALE
        for i in range(nb):
            sel = [None] * i
            if i > TOPK:
                rows = [g[:, i * BLK - first_q:(i + 1) * BLK - first_q] for g in gate[:i]]
                for j in range(i):
                    rank = jnp.zeros((1, BLK), F32)
                    for jj in range(i):
                        if jj != j:
                            beats = (rows[jj] >= rows[j]) if jj < j else (rows[jj] > rows[j])
                            rank = rank + jnp.where(beats, 1.0, 0.0)
                    sel[j] = jnp.where(rank < TOPK, 0.0, NEG_INF)
            qblk = qs[:, i * BLK:(i + 1) * BLK]
            qpad = jnp.concatenate([qblk, zeros_q] if hh == 0 else [zeros_q, qblk], axis=0).astype(BF16)
            m = None
            for j in range(i + 1):
                sc = _dot(k_scr[j], qpad) + bias_ref[hh, i - j]
                if j < i and sel[j] is not None:
                    sc = sc + sel[j]
                sc_scr[j] = sc
                mj = jnp.max(sc, axis=0, keepdims=True)
                m = mj if m is None else jnp.maximum(m, mj)
            for j in range(i + 1):
                p_scr[j * BLK:(j + 1) * BLK, :] = jnp.exp(sc_scr[j] - m).astype(BF16)
            acc = _dot(va_scr[hh, :, 0:(i + 1) * BLK], p_scr[0:(i + 1) * BLK, :])
            out = acc[0:HD] / acc[HD:HD + 1]
            o_ref[0, hh * HD:(hh + 1) * HD, i * BLK:(i + 1) * BLK] = out.astype(BF16)


def _attention(projt, bias, qg, kg):
    bsz, _, s = projt.shape
    nb = s // BLK
    rows = 2 * HD
    qb, kb, vb = Q0 // rows, K0 // rows, V0 // rows
    return pl.pallas_call(
        functools.partial(_attn_body, s=s),
        grid=(H // 2, bsz),
        in_specs=[pl.BlockSpec((1, rows, s), lambda p, b: (b, qb + p, 0)),
                  pl.BlockSpec((1, rows, s), lambda p, b: (b, kb + p, 0)),
                  pl.BlockSpec((1, rows, s), lambda p, b: (b, vb + p, 0)),
                  pl.BlockSpec((2, nb, BLK, BLK), lambda p, b: (p, 0, 0, 0)),
                  pl.BlockSpec((HD, 1), lambda p, b: (0, 0)),
                  pl.BlockSpec((HD, 1), lambda p, b: (0, 0))],
        out_specs=pl.BlockSpec((1, rows, s), lambda p, b: (b, p, 0)),
        out_shape=jax.ShapeDtypeStruct((bsz, ATTN_W, s), BF16),
        scratch_shapes=[pltpu.VMEM((nb, BLK, rows), BF16),
                        pltpu.VMEM((nb, BLK, BLK), F32),
                        pltpu.VMEM((s, BLK), BF16),
                        pltpu.VMEM((2, VA_ROWS, s), BF16)],
        compiler_params=_cparams("arbitrary", "arbitrary"),
        name="moba_attention",
    )(projt, projt, projt, bias, qg, kg)


def _split3(x):
    hi = x.astype(BF16)
    r1 = x - hi.astype(F32)
    mid = r1.astype(BF16)
    lo = (r1 - mid.astype(F32)).astype(BF16)
    return hi, mid, lo


def _ssd_body(z_ref, xs_ref, bc_ref, dtt_ref, dt_ref, cw_ref, cb_ref, dtbc_ref, dtbr_ref, alc_ref, alr_ref,
              dskip_ref, ng_ref, o_ref, prevx_scr, state_scr):
    @pl.when(pl.program_id(1) == 0)
    def _():
        prevx_scr[...] = jnp.zeros_like(prevx_scr)
        state_scr[...] = jnp.zeros_like(state_scr)

    cur = jnp.concatenate([xs_ref[0], bc_ref[0]], axis=0).astype(F32)
    prev = prevx_scr[...]
    lane = lax.broadcasted_iota(I32, (1, L), 1)
    acc = cb_ref[...] + cw_ref[SSM_K - 1] * cur
    for sft in range(1, SSM_K):
        shifted = pltpu.roll(jnp.where(lane >= L - sft, prev, cur), sft, 1)
        acc = acc + cw_ref[SSM_K - 1 - sft] * shifted
    prevx_scr[...] = cur
    xbc = acc * _sigmoid(acc)

    dtt = _softplus(dtt_ref[0] + dtbc_ref[...])
    dts = _softplus(dt_ref[0] + dtbr_ref[...])
    adtt = dtt * (-jnp.exp(alc_ref[...]))
    adts = dts * (-jnp.exp(alr_ref[...]))
    r_i = lax.broadcasted_iota(I32, (L, L), 0)
    c_i = lax.broadcasted_iota(I32, (L, L), 1)
    upper = jnp.where(r_i <= c_i, 1.0, 0.0).astype(BF16)
    lower = jnp.where(r_i >= c_i, 1.0, 0.0).astype(BF16)
    acst = sum(_dot(part, upper) for part in _split3(adtt))
    acs = sum(_dot(lower, part) for part in _split3(adts))
    causal = c_i >= r_i
    last = acst[:, L - 1:L]
    dte = jnp.exp(last - acst)
    cdec = jnp.broadcast_to(jnp.exp(last), (SSM_H, L))
    ein = jnp.exp(acst)

    for g in range(SSM_G):
        bt = xbc[SSM_IN + g * SSM_N:SSM_IN + (g + 1) * SSM_N]
        ct = xbc[SSM_IN + SSM_G * SSM_N + g * SSM_N:SSM_IN + SSM_G * SSM_N + (g + 1) * SSM_N]
        ctb = ct.astype(BF16)
        bmb = bt.T.astype(BF16)
        cbt = _dot(bmb, ctb)
        rows = slice(g * SSM_J * SSM_HD, (g + 1) * SSM_J * SSM_HD)
        xg = xbc[rows]
        state = state_scr[rows]
        yoff = _dot(state.astype(BF16), ctb)
        xdt, sin, cd, ys = [], [], [], []
        for j in range(SSM_J):
            h = g * SSM_J + j
            xj = xg[j * SSM_HD:(j + 1) * SSM_HD]
            xdj = xj * dtt[h:h + 1]
            xdt.append(xdj)
            sin.append(xdj * dte[h:h + 1])
            cd.append(jnp.broadcast_to(cdec[h:h + 1], (SSM_HD, SSM_N)))
        new_state = _dot(jnp.concatenate(sin, axis=0).astype(BF16), bmb)
        state_scr[rows] = state * jnp.concatenate(cd, axis=0) + new_state
        for j in range(SSM_J):
            h = g * SSM_J + j
            seg = acst[h:h + 1] - acs[:, h:h + 1]
            mt = cbt * jnp.exp(jnp.where(causal, seg, NEG_INF))
            ydiag = _dot(xdt[j].astype(BF16), mt.astype(BF16))
            ys.append(ydiag + yoff[j * SSM_HD:(j + 1) * SSM_HD] * ein[h:h + 1])
        y = jnp.concatenate(ys, axis=0) + dskip_ref[rows] * xg
        zg = z_ref[0, rows].astype(F32)
        y = y * (zg * _sigmoid(zg))
        ms = jnp.mean(y * y, axis=0, keepdims=True)
        o_ref[0, rows] = (y * lax.rsqrt(ms + EPS) * ng_ref[rows]).astype(BF16)


def _ssd(projt, dtt, dts, cw, cb, dtb, alog, dskip, ng):
    bsz, _, s = projt.shape
    nc = s // L
    full = lambda shape: pl.BlockSpec(shape, lambda b, c: (0,) * len(shape))
    return pl.pallas_call(
        _ssd_body,
        grid=(bsz, nc),
        in_specs=[pl.BlockSpec((1, SSM_IN, L), lambda b, c: (b, Z0 // SSM_IN, c)),
                  pl.BlockSpec((1, SSM_IN, L), lambda b, c: (b, X0 // SSM_IN, c)),
                  pl.BlockSpec((1, 2 * SSM_G * SSM_N, L), lambda b, c: (b, BC0 // (2 * SSM_G * SSM_N), c)),
                  pl.BlockSpec((1, SSM_H, L), lambda b, c: (b, 0, c)),
                  pl.BlockSpec((1, L, SSM_H), lambda b, c: (b, c, 0)),
                  full((SSM_K, CONV_DIM, L)), full((CONV_DIM, L)),
                  full((SSM_H, 1)), full((1, SSM_H)), full((SSM_H, 1)), full((1, SSM_H)),
                  full((SSM_IN, L)), full((SSM_IN, L))],
        out_specs=pl.BlockSpec((1, SSM_IN, L), lambda b, c: (b, 0, c)),
        out_shape=jax.ShapeDtypeStruct((bsz, SSM_IN, s), BF16),
        scratch_shapes=[pltpu.VMEM((CONV_DIM, L), F32), pltpu.VMEM((SSM_IN, SSM_N), F32)],
        compiler_params=_cparams("arbitrary", "arbitrary"),
        name="ssd_branch",
    )(projt, projt, projt, dtt, dts, cw, cb, dtb.reshape(SSM_H, 1), dtb.reshape(1, SSM_H),
      alog.reshape(SSM_H, 1), alog.reshape(1, SSM_H), dskip, ng)


def _merge_body(attn_ref, ssm_ref, gate_ref, x_ref, mod_ref, gb_ref, wat_ref, wst_ref, wo_ref, g2_ref,
                rwt_ref, rb_ref, x1_ref, u2_ref, lg_ref):
    a = _dot(wat_ref[...], attn_ref[0])
    sm = _dot(wst_ref[...], ssm_ref[0])
    gate = _sigmoid(gate_ref[0].astype(F32) + gb_ref[...])
    merged_t = gate[:D] * a + gate[D:] * sm
    merged = merged_t.T.astype(BF16)
    y = _dot(merged, wo_ref[...])
    x1 = x_ref[0] + mod_ref[0, 2:3, :] * y
    x1_ref[0] = x1
    ms = jnp.mean(x1 * x1, axis=-1, keepdims=True)
    u2 = (x1 * lax.rsqrt(ms + EPS) * g2_ref[...]) * (1.0 + mod_ref[0, 4:5, :]) + mod_ref[0, 3:4, :]
    u2_ref[0] = u2
    lg_ref[...] = _dot_nt(rwt_ref[...], u2.astype(BF16)) + rb_ref[...]


def _merge(attnt, ssmt, projt, x, mod, gb, wat, wst, wo, g2, rwt, rb):
    bsz, s, _ = x.shape
    tm = 512
    nm = s // tm
    full = lambda shape: pl.BlockSpec(shape, lambda b, m: (0,) * len(shape))
    return pl.pallas_call(
        _merge_body,
        grid=(bsz, nm),
        in_specs=[pl.BlockSpec((1, ATTN_W, tm), lambda b, m: (b, 0, m)),
                  pl.BlockSpec((1, SSM_IN, tm), lambda b, m: (b, 0, m)),
                  pl.BlockSpec((1, 2 * D, tm), lambda b, m: (b, G0 // (2 * D), m)),
                  pl.BlockSpec((1, tm, D), lambda b, m: (b, m, 0)),
                  pl.BlockSpec((1, 6, D), lambda b, m: (b, 0, 0)),
                  full((2 * D, 1)), full((D, ATTN_W)), full((D, SSM_IN)), full((D, D)), full((1, D)),
                  full((NE, D)), full((NE, 1))],
        out_specs=[pl.BlockSpec((1, tm, D), lambda b, m: (b, m, 0)),
                   pl.BlockSpec((1, tm, D), lambda b, m: (b, m, 0)),
                   pl.BlockSpec((NE, tm), lambda b, m: (0, b * nm + m))],
        out_shape=[jax.ShapeDtypeStruct((bsz, s, D), F32),
                   jax.ShapeDtypeStruct((bsz, s, D), F32),
                   jax.ShapeDtypeStruct((NE, bsz * s), F32)],
        compiler_params=_cparams("arbitrary", "arbitrary"),
        name="merge_outproj",
    )(attnt, ssmt, projt, x, mod, gb, wat, wst, wo, g2, rwt, rb)


def _route_body(lg_ref, e_ref, w_ref, r_ref, cnt_ref, carry_scr, *, tm):
    @pl.when(pl.program_id(0) == 0)
    def _():
        carry_scr[...] = jnp.zeros_like(carry_scr)

    work = lg_ref[...]
    eidx = lax.broadcasted_iota(I32, (NE, tm), 0).astype(F32)
    vals, idxs, hots = [], [], []
    for _ in range(TOPE):
        m = jnp.max(work, axis=0, keepdims=True)
        idx = jnp.min(jnp.where(work == m, eidx, float(NE)), axis=0, keepdims=True)
        hot = eidx == idx
        work = jnp.where(hot, NEG_INF, work)
        vals.append(m)
        idxs.append(idx)
        hots.append(hot)
    ex = [jnp.exp(v - vals[0]) for v in vals]
    den = ex[0] + ex[1] + ex[2] + ex[3]
    multi = jnp.zeros((NE, tm), F32)
    for hot in hots:
        multi = multi + jnp.where(hot, 1.0, 0.0)
    r_i = lax.broadcasted_iota(I32, (tm, tm), 0)
    c_i = lax.broadcasted_iota(I32, (tm, tm), 1)
    strict = jnp.where(r_i < c_i, 1.0, 0.0).astype(BF16)
    before = _dot(multi.astype(BF16), strict) + carry_scr[:, 0:1]
    for k in range(TOPE):
        e_ref[k:k + 1, :] = idxs[k].astype(I32)
        w_ref[k:k + 1, :] = ex[k] / den
        r_ref[k:k + 1, :] = jnp.sum(jnp.where(hots[k], before, 0.0), axis=0, keepdims=True).astype(I32)
    total = carry_scr[...] + jnp.sum(multi, axis=1, keepdims=True)
    carry_scr[...] = total
    cnt_ref[...] = total


def _route(logits_t, t0, th):
    tm = 512
    off = t0 // tm
    return pl.pallas_call(
        functools.partial(_route_body, tm=tm),
        grid=(th // tm,),
        in_specs=[pl.BlockSpec((NE, tm), lambda i: (0, off + i))],
        out_specs=[pl.BlockSpec((TOPE, tm), lambda i: (0, i)),
                   pl.BlockSpec((TOPE, tm), lambda i: (0, i)),
                   pl.BlockSpec((TOPE, tm), lambda i: (0, i)),
                   pl.BlockSpec((NE, 128), lambda i: (0, 0))],
        out_shape=[jax.ShapeDtypeStruct((TOPE, th), I32),
                   jax.ShapeDtypeStruct((TOPE, th), F32),
                   jax.ShapeDtypeStruct((TOPE, th), I32),
                   jax.ShapeDtypeStruct((NE, 128), F32)],
        scratch_shapes=[pltpu.VMEM((NE, 128), F32)],
        compiler_params=_cparams("arbitrary"),
        name="route_top4",
    )(logits_t)


class _Plan(NamedTuple):
    dest: jax.Array
    top_w: jax.Array
    p_end: jax.Array
    block_e: jax.Array
    n_used: jax.Array
    cap: int


def _plan(logits_t, t0, th):
    top_e, top_w, rank, cnt = _route(logits_t, t0, th)
    counts = cnt[:, 0].astype(I32)
    padded = (counts + MOE_ROWS - 1) // MOE_ROWS * MOE_ROWS
    p_end = jnp.cumsum(padded).astype(I32)
    p_start = p_end - padded
    experts = jnp.arange(NE, dtype=I32)
    dest = rank + jnp.sum(jnp.where(top_e[..., None] == experts, p_start, 0), axis=-1)
    cap = TOPE * th + NE * MOE_ROWS
    blk_start = jnp.arange(cap // MOE_ROWS, dtype=I32) * MOE_ROWS
    block_e = jnp.minimum(jnp.sum((p_end[None, :] <= blk_start[:, None]).astype(I32), axis=1), NE - 1)
    n_used = (p_end[-1] // MOE_ROWS).astype(I32).reshape(1)
    return _Plan(dest, top_w.T, p_end, block_e, n_used, cap)


def _dest_blocks(dest, tm):
    return dest.reshape(TOPE, dest.shape[1] // tm, tm).transpose(1, 0, 2)


def _row_copy(src, src_row, dst, dst_row, sem):
    return pltpu.make_async_copy(src.at[pl.ds(src_row, 1)], dst.at[pl.ds(dst_row, 1)], sem)


def _zero_fill(pe_ref, nu_ref, o_hbm, zero_scr, sem, nblk):
    zero_scr[...] = jnp.zeros_like(zero_scr)
    parts = MOE_ROWS // ZERO_ROWS

    def fill(blk, part):
        return pltpu.make_async_copy(zero_scr, o_hbm.at[pl.ds(blk * MOE_ROWS + part * ZERO_ROWS, ZERO_ROWS)], sem)

    pad_blocks = []
    for e in range(NE):
        prev_end = pe_ref[e - 1] if e > 0 else 0
        pad_blocks.append((pe_ref[e] > prev_end, pe_ref[e] // MOE_ROWS - 1))
    for b in range(NE):
        pad_blocks.append((nblk - 1 - b >= nu_ref[0], nblk - 1 - b))
    for cond, blk in pad_blocks:
        @pl.when(cond)
        def _(blk=blk):
            for part in range(parts):
                fill(blk, part).start()
    for cond, blk in pad_blocks:
        @pl.when(cond)
        def _(blk=blk):
            for part in range(parts):
                fill(blk, part).wait()


def _dispatch_body(pe_ref, nu_ref, dest_ref, u_ref, o_hbm, zero_scr, sem, *, nblk):
    @pl.when(pl.program_id(0) == 0)
    def _():
        _zero_fill(pe_ref, nu_ref, o_hbm, zero_scr, sem, nblk)

    def start(r, c):
        for k in range(TOPE):
            _row_copy(u_ref, r, o_hbm, dest_ref[0, k, r], sem).start()
        return c

    def wait(r, c):
        for k in range(TOPE):
            _row_copy(u_ref, 0, o_hbm, 0, sem).wait()
        return c

    lax.fori_loop(0, COMBINE_TOK, start, 0)
    lax.fori_loop(0, COMBINE_TOK, wait, 0)


def _dispatch(u2, plan, tile0):
    tm = COMBINE_TOK
    th = plan.dest.shape[1]
    nblk = plan.cap // MOE_ROWS
    grid_spec = pltpu.PrefetchScalarGridSpec(
        num_scalar_prefetch=2,
        grid=(th // tm,),
        in_specs=[pl.BlockSpec((1, TOPE, tm), lambda i, pe, nu: (i, 0, 0), memory_space=pltpu.SMEM),
                  pl.BlockSpec((tm, D), lambda i, pe, nu: (tile0 + i, 0))],
        out_specs=pl.BlockSpec(memory_space=pl.ANY),
        scratch_shapes=[pltpu.VMEM((ZERO_ROWS, D), F32), pltpu.SemaphoreType.DMA(())],
    )
    return pl.pallas_call(
        functools.partial(_dispatch_body, nblk=nblk),
        grid_spec=grid_spec,
        out_shape=jax.ShapeDtypeStruct((plan.cap, D), F32),
        compiler_params=_cparams("arbitrary"),
        name="moe_dispatch",
    )(plan.p_end, plan.n_used, _dest_blocks(plan.dest, tm), u2)


def _expert_body(be_ref, nu_ref, x_ref, w1_ref, b1_ref, w2_ref, b2_ref, o_ref, w1_scr, w2_scr):
    i = pl.program_id(0)
    changed = jnp.logical_or(i == 0, be_ref[i] != be_ref[jnp.maximum(i - 1, 0)])

    @pl.when(changed)
    def _():
        w1_scr[...] = w1_ref[0].astype(BF16)
        w2_scr[...] = w2_ref[0].astype(BF16)

    @pl.when(i < nu_ref[0])
    def _():
        h = _dot(x_ref[...].astype(BF16), w1_scr[...]) + b1_ref[0]
        glu = jnp.minimum(h[:, :FF], LIMIT)
        lin = jnp.clip(h[:, FF:], -LIMIT, LIMIT)
        act = glu * _sigmoid(ALPHA * glu) * (lin + 1.0)
        o_ref[...] = _dot(act.astype(BF16), w2_scr[...]) + b2_ref[0]

    @pl.when(i >= nu_ref[0])
    def _():
        o_ref[...] = jnp.zeros_like(o_ref)


def _experts(xs, plan, w1, b1, w2, b2):
    cap = xs.shape[0]
    nblk = cap // MOE_ROWS
    grid_spec = pltpu.PrefetchScalarGridSpec(
        num_scalar_prefetch=2,
        grid=(nblk,),
        in_specs=[pl.BlockSpec((MOE_ROWS, D), lambda i, be, nu: (i, 0)),
                  pl.BlockSpec((1, D, 2 * FF), lambda i, be, nu: (be[i], 0, 0)),
                  pl.BlockSpec((1, 1, 2 * FF), lambda i, be, nu: (be[i], 0, 0)),
                  pl.BlockSpec((1, FF, D), lambda i, be, nu: (be[i], 0, 0)),
                  pl.BlockSpec((1, 1, D), lambda i, be, nu: (be[i], 0, 0))],
        out_specs=pl.BlockSpec((MOE_ROWS, D), lambda i, be, nu: (i, 0)),
        scratch_shapes=[pltpu.VMEM((D, 2 * FF), BF16), pltpu.VMEM((FF, D), BF16)],
    )
    return pl.pallas_call(
        _expert_body,
        grid_spec=grid_spec,
        out_shape=jax.ShapeDtypeStruct((cap, D), F32),
        compiler_params=_cparams("arbitrary"),
        name="moe_experts",
    )(plan.block_e, plan.n_used, xs, w1, b1.reshape(NE, 1, 2 * FF), w2, b2.reshape(NE, 1, D))


def _combine_body(dest_ref, ys_hbm, w_ref, x1_ref, g2_ref, o_ref, buf, sem):
    def start(r, c):
        for k in range(TOPE):
            _row_copy(ys_hbm, dest_ref[0, k, r], buf.at[k], r, sem).start()
        return c

    def wait(r, c):
        for k in range(TOPE):
            _row_copy(ys_hbm, 0, buf.at[k], 0, sem).wait()
        return c

    lax.fori_loop(0, COMBINE_TOK, start, 0)
    lax.fori_loop(0, COMBINE_TOK, wait, 0)
    w = w_ref[...]
    ffn = buf[0] * w[:, 0:1]
    for k in range(1, TOPE):
        ffn = ffn + buf[k] * w[:, k:k + 1]
    o_ref[0] = x1_ref[0] + g2_ref[0] * ffn


def _combine(ys, plan, x1, gate2, b0):
    s = x1.shape[1]
    tm = COMBINE_TOK
    nm = s // tm
    bh = plan.dest.shape[1] // s
    return pl.pallas_call(
        _combine_body,
        grid=(bh, nm),
        in_specs=[pl.BlockSpec((1, TOPE, tm), lambda b, m: (b * nm + m, 0, 0), memory_space=pltpu.SMEM),
                  pl.BlockSpec(memory_space=pl.ANY),
                  pl.BlockSpec((tm, TOPE), lambda b, m: (b * nm + m, 0)),
                  pl.BlockSpec((1, tm, D), lambda b, m: (b0 + b, m, 0)),
                  pl.BlockSpec((1, 1, D), lambda b, m: (b0 + b, 0, 0))],
        out_specs=pl.BlockSpec((1, tm, D), lambda b, m: (b, m, 0)),
        out_shape=jax.ShapeDtypeStruct((bh, s, D), F32),
        scratch_shapes=[pltpu.VMEM((TOPE, tm, D), F32), pltpu.SemaphoreType.DMA(())],
        compiler_params=_cparams("arbitrary", "arbitrary"),
        name="moe_combine",
    )(_dest_blocks(plan.dest, tm), ys, plan.top_w, x1, gate2)


def _layer(x, mod, norm1_g, w_in, q_norm_g, k_norm_g, rel_bias_table, conv_w, conv_b, dt_bias, a_log, d_skip,
           ssm_norm_g, w_attn_branch, w_ssm_branch, gate_bias, w_out, norm2_g, router_w, router_b,
           expert_w1, expert_b1, expert_w2, expert_b2):
    bsz, s, _ = x.shape
    t = bsz * s
    assert s % BLK == 0 and s % 512 == 0
    mod3 = mod.reshape(bsz, 6, D)

    c0 = 3 * ATTN_W
    c_x, c_b, c_dt, c_g = c0 + SSM_IN, c0 + SSM_IN + SSM_IN, c0 + SSM_IN + CONV_DIM, c0 + SSM_IN + CONV_DIM + SSM_H
    w_rows = jnp.concatenate([w_in[:, c0:c_x], w_in[:, c_g:], w_in[:, c_x:c_b], w_in[:, :c0], w_in[:, c_b:c_dt]],
                             axis=1)
    wt = w_rows.T.astype(BF16)
    wdt = w_in[:, c_dt:c_g].astype(BF16)
    projt, dtt, dts = _inproj(x, mod3, norm1_g.reshape(1, D), wt, wdt.T, wdt)

    bias = _bias_tiles(rel_bias_table.T.astype(F32), s // BLK)
    attnt = _attention(projt, bias, q_norm_g.reshape(HD, 1), k_norm_g.reshape(HD, 1))

    lanes = lambda v: jnp.broadcast_to(v[..., None], v.shape + (L,))
    ssmt = _ssd(projt, dtt, dts, lanes(conv_w), lanes(conv_b), dt_bias, a_log,
                lanes(jnp.repeat(d_skip, SSM_HD)), lanes(ssm_norm_g))

    x1, u2, logits_t = _merge(attnt, ssmt, projt, x, mod3, gate_bias.reshape(2 * D, 1),
                              w_attn_branch.T.astype(BF16), w_ssm_branch.T.astype(BF16), w_out.astype(BF16),
                              norm2_g.reshape(1, D), router_w.T.astype(BF16), router_b.reshape(NE, 1))

    plan = _plan(logits_t, 0, t)
    xs = _dispatch(u2.reshape(t, D), plan, 0)
    ys = _experts(xs, plan, expert_w1, expert_b1, expert_w2, expert_b2)
    return _combine(ys, plan, x1, mod3[:, 5:6, :], 0)


def kernel(x, c, ada_w, ada_b, norm1_g, w_in, q_norm_g, k_norm_g, rel_bias_table, conv_w, conv_b, dt_bias, a_log,
           d_skip, ssm_norm_g, w_attn_branch, w_ssm_branch, gate_bias, w_out, norm2_g, router_w, router_b,
           expert_w1, expert_b1, expert_w2, expert_b2):
    h = x
    for l in range(ada_w.shape[0]):
        mod = _ada(c, ada_w[l], ada_b[l])
        h = _layer(h, mod, norm1_g[l], w_in[l], q_norm_g[l], k_norm_g[l], rel_bias_table, conv_w[l], conv_b[l],
                   dt_bias[l], a_log[l], d_skip[l], ssm_norm_g[l], w_attn_branch[l], w_ssm_branch[l],
                   gate_bias[l], w_out[l], norm2_g[l], router_w[l], router_b[l], expert_w1[l], expert_b1[l],
                   expert_w2[l], expert_b2[l])
    return h
```
